```python
import math
import numpy as np
import jax
import jax.numpy as jnp
from jax import lax


D_MODEL = 1024
BATCH = 16
SEQ = 2048
DEPTH = 2

GRID_W = 64
CTX_LEN = 256
N_BRANCH = 4
BRANCH_W = 512
QBLK = 128
ROPE_THETA = 10000.0
EPS = 1e-6
NA_HEADS = 8
NA_HEAD_DIM = 64
NA_ROW_WIN = 8
NA_COL_WIN = 16
MLA_HEADS = 8
MLA_Q_RANK = 384
MLA_KV_RANK = 256
MLA_NOPE = 64
MLA_ROPE = 32
MLA_V = 64
GQA_Q_HEADS = 8
GQA_KV_HEADS = 2
GQA_HEAD_DIM = 64
SSM_HEADS = 8
SSM_HEAD_DIM = 64
SSM_GROUPS = 2
SSM_STATE = 128
SSM_CONV = 5
SSM_CHUNK = 128
SSM_INNER = SSM_HEADS * SSM_HEAD_DIM
SSM_BC = SSM_GROUPS * SSM_STATE
SSM_CONV_CH = SSM_INNER + 2 * SSM_BC
FFN_HIDDEN = ((8 * D_MODEL + 3 * 256 - 1) // (3 * 256)) * 256

KEY_SPLITS = (('na_k', NA_HEADS * NA_HEAD_DIM), ('na_v', NA_HEADS * NA_HEAD_DIM),
              ('mla_ckv', MLA_KV_RANK), ('mla_kr', MLA_ROPE),
              ('gqa_k', GQA_KV_HEADS * GQA_HEAD_DIM), ('gqa_v', GQA_KV_HEADS * GQA_HEAD_DIM),
              ('ssm_x', SSM_INNER), ('ssm_B', SSM_BC), ('ssm_dt', 2 * SSM_HEADS))
QUERY_SPLITS = (('na_q', NA_HEADS * NA_HEAD_DIM), ('mla_cq', MLA_Q_RANK),
                ('gqa_q', GQA_Q_HEADS * GQA_HEAD_DIM), ('ssm_C', SSM_BC), ('ssm_z', SSM_INNER))
KEY_COLS = sum(w for _, w in KEY_SPLITS)
MIX_COLS = KEY_COLS + sum(w for _, w in QUERY_SPLITS)
IN_COLS = MIX_COLS + N_BRANCH * D_MODEL

kernel_name = "hybrid_gated_na_mla_gqa_ssd_prefix_dit"


def rmsnorm(x, g):
    xf = x.astype(jnp.float32)
    y = xf * lax.rsqrt(jnp.mean(xf * xf, axis=-1, keepdims=True) + EPS) * g.astype(jnp.float32)
    return y.astype(x.dtype)


def split_cols(p, with_query):
    names = KEY_SPLITS + (QUERY_SPLITS if with_query else ())
    out, off = {}, 0
    for name, w in names:
        out[name] = p[..., off:off + w]
        off += w
    return out


def to_heads(a, nh):
    b, t, w = a.shape
    return a.reshape(b, t, nh, w // nh).transpose(0, 2, 1, 3)


def merge_heads(o):
    b, g, r, t, d = o.shape
    return o.transpose(0, 3, 1, 2, 4).reshape(b, t, g * r * d)


def rope_tables(n, dim):
    t = jnp.arange(n)
    row = (t // GRID_W).astype(jnp.float32)
    col = (t % GRID_W).astype(jnp.float32)
    quarter = dim // 4
    inv = ROPE_THETA ** (-jnp.arange(quarter, dtype=jnp.float32) / quarter)
    ang = jnp.concatenate([row[:, None] * inv, col[:, None] * inv], axis=-1)
    return jnp.cos(ang), jnp.sin(ang)


def apply_rope(x, cos, sin):
    half = x.shape[-1] // 2
    x1, x2 = x[..., :half], x[..., half:]
    cos = cos.astype(x.dtype)
    sin = sin.astype(x.dtype)
    return jnp.concatenate([x1 * cos - x2 * sin, x1 * sin + x2 * cos], axis=-1)


def block_attention(q, k, v, scale):
    b, g, r, t, dk = q.shape
    nb = t // QBLK
    qb = jnp.moveaxis(q.reshape(b, g, r, nb, QBLK, dk), 3, 0)

    def one(qi):
        s = jnp.einsum('bgrqd,bgud->bgrqu', qi, k).astype(jnp.float32) * scale
        p = jax.nn.softmax(s, axis=-1).astype(v.dtype)
        return jnp.einsum('bgrqu,bgud->bgrqd', p, v)

    o = lax.map(one, qb)
    return jnp.moveaxis(o, 0, 3).reshape(b, g, r, t, v.shape[-1])


def na_mixer(pl, pc, rpb, need_ctx):
    bsz, s, _ = pl['na_q'].shape
    rows = s // GRID_W
    kr = min(NA_ROW_WIN, rows)
    nh, dh, kcw = NA_HEADS, NA_HEAD_DIM, NA_COL_WIN
    scale = dh ** -0.5

    def grid(a):
        return a.reshape(bsz, rows, GRID_W, nh, dh).transpose(0, 3, 1, 2, 4)

    qg, kg, vg = grid(pl['na_q']), grid(pl['na_k']), grid(pl['na_v'])
    kc, vc = to_heads(pc['na_k'], nh), to_heads(pc['na_v'], nh)
    col_start = np.clip(np.arange(GRID_W) - kcw // 2, 0, GRID_W - kcw)
    col_idx = col_start[:, None] + np.arange(kcw)[None, :]
    col_off = col_idx - np.arange(GRID_W)[:, None] + kcw - 1
    rpb_cols = rpb[:, :, col_off].astype(jnp.float32)
    n_loc = kr * kcw

    def one_row(r):
        rs = jnp.clip(r - kr // 2, 0, rows - kr)
        q_r = lax.dynamic_index_in_dim(qg, r, axis=2, keepdims=False)
        k_win = lax.dynamic_slice_in_dim(kg, rs, kr, axis=2)[:, :, :, col_idx]
        v_win = lax.dynamic_slice_in_dim(vg, rs, kr, axis=2)[:, :, :, col_idx]
        bias = jnp.take(rpb_cols, rs + jnp.arange(kr) - r + NA_ROW_WIN - 1, axis=1)
        s_loc = jnp.einsum('bhwd,bhiwjd->bhwij', q_r, k_win).astype(jnp.float32) * scale + jnp.transpose(bias, (0, 2, 1, 3))
        s_ctx = jnp.einsum('bhwd,bhld->bhwl', q_r, kc).astype(jnp.float32) * scale
        p = jax.nn.softmax(jnp.concatenate([s_loc.reshape(bsz, nh, GRID_W, n_loc), s_ctx], axis=-1), axis=-1).astype(vg.dtype)
        return (jnp.einsum('bhwij,bhiwjd->bhwd', p[..., :n_loc].reshape(bsz, nh, GRID_W, kr, kcw), v_win)
                + jnp.einsum('bhwl,bhld->bhwd', p[..., n_loc:], vc))

    o = lax.map(one_row, jnp.arange(rows))
    y = o.transpose(1, 0, 3, 2, 4).reshape(bsz, s, nh * dh)
    yc = None
    if need_ctx:
        yc = merge_heads(block_attention(to_heads(pc['na_q'], nh)[:, :, None], kc, vc, scale))
    return y, yc


def mla_mixer(pl, pc, g_q, g_kv, w_uq, w_ukv, rope, need_ctx):
    cos, sin = rope
    scale = (MLA_NOPE + MLA_ROPE) ** -0.5

    def queries(p, rotate):
        q = to_heads(rmsnorm(p['mla_cq'], g_q) @ w_uq, MLA_HEADS)
        q_nope, q_pe = q[..., :MLA_NOPE], q[..., MLA_NOPE:]
        if rotate:
            q_pe = apply_rope(q_pe, cos, sin)
        return jnp.concatenate([q_nope, q_pe], axis=-1)[:, :, None]

    def keys_values(p, rotate):
        kv = to_heads(rmsnorm(p['mla_ckv'], g_kv) @ w_ukv, MLA_HEADS)
        k_nope, v = kv[..., :MLA_NOPE], kv[..., MLA_NOPE:]
        k_pe = p['mla_kr'][:, None]
        if rotate:
            k_pe = apply_rope(k_pe, cos, sin)
        k = jnp.concatenate([k_nope, jnp.broadcast_to(k_pe, k_nope.shape[:-1] + (MLA_ROPE,))], axis=-1)
        return k, v

    kl, vl = keys_values(pl, True)
    kc, vc = keys_values(pc, False)
    y = merge_heads(block_attention(queries(pl, True), jnp.concatenate([kc, kl], axis=2),
                                    jnp.concatenate([vc, vl], axis=2), scale))
    yc = None
    if need_ctx:
        yc = merge_heads(block_attention(queries(pc, False), kc, vc, scale))
    return y, yc


def gqa_mixer(pl, pc, g_q, g_k, rope, need_ctx):
    cos, sin = rope
    rep = GQA_Q_HEADS // GQA_KV_HEADS
    scale = GQA_HEAD_DIM ** -0.5

    def queries(p, rotate):
        q = rmsnorm(to_heads(p['gqa_q'], GQA_Q_HEADS), g_q)
        if rotate:
            q = apply_rope(q, cos, sin)
        b, _, t, d = q.shape
        return q.reshape(b, GQA_KV_HEADS, rep, t, d)

    def keys_values(p, rotate):
        k = rmsnorm(to_heads(p['gqa_k'], GQA_KV_HEADS), g_k)
        if rotate:
            k = apply_rope(k, cos, sin)
        return k, to_heads(p['gqa_v'], GQA_KV_HEADS)

    kl, vl = keys_values(pl, True)
    kc, vc = keys_values(pc, False)
    y = merge_heads(block_attention(queries(pl, True), jnp.concatenate([kc, kl], axis=2),
                                    jnp.concatenate([vc, vl], axis=2), scale))
    yc = None
    if need_ctx:
        yc = merge_heads(block_attention(queries(pc, False), kc, vc, scale))
    return y, yc


def dwconv_centred(x, w, b):
    k = w.shape[0]
    y = lax.conv_general_dilated(x, w[:, None, :], window_strides=(1,), padding=[(k // 2, k // 2)],
                                 dimension_numbers=('NWC', 'WIO', 'NWC'), feature_group_count=x.shape[-1])
    return y + b


def ssd(x, dt, a, bm, cm, h0):
    bsz, t, nh, hp = x.shape
    ng, ns = bm.shape[2], bm.shape[3]
    r = nh // ng
    q = SSM_CHUNK
    nc = t // q
    f32 = jnp.float32
    xc = x.astype(f32).reshape(bsz, nc, q, ng, r, hp)
    dtc = dt.reshape(bsz, nc, q, ng, r)
    bc = bm.astype(f32).reshape(bsz, nc, q, ng, ns)
    a_cum = jnp.cumsum(dtc * a.reshape(ng, r), axis=2)
    a_tot = a_cum[:, :, -1]
    xdt = xc * dtc[..., None]
    s_loc = jnp.einsum('bcsgn,bcsgr,bcsgrp->bcgrpn', bc, jnp.exp(a_tot[:, :, None] - a_cum), xdt)

    def step(h, inp):
        s_c, at = inp
        return jnp.exp(at)[..., None, None] * h + s_c, h

    h_fin, h_in = lax.scan(step, h0.reshape(bsz, ng, r, hp, ns),
                           (jnp.moveaxis(s_loc, 1, 0), jnp.moveaxis(a_tot, 1, 0)))
    h_fin = h_fin.reshape(bsz, nh, hp, ns)
    if cm is None:
        return None, h_fin
    cc = cm.astype(f32).reshape(bsz, nc, q, ng, ns)
    h_in = jnp.moveaxis(h_in, 0, 1)
    lower = np.tril(np.ones((q, q), dtype=bool))[:, :, None, None]
    seg = a_cum[:, :, :, None] - a_cum[:, :, None, :]
    lmat = jnp.exp(jnp.where(lower, seg, -jnp.inf))
    cb = jnp.einsum('bcqgn,bcsgn->bcqsg', cc, bc)
    y = (jnp.einsum('bcqsg,bcqsgr,bcsgrp->bcqgrp', cb, lmat, xdt)
         + jnp.einsum('bcqgn,bcgrpn->bcqgrp', cc, h_in) * jnp.exp(a_cum)[..., None])
    return y.reshape(bsz, t, nh, hp).astype(x.dtype), h_fin


def ssm_mixer(pl, pc, conv_w, conv_b, a_log, dt_bias, d_skip, g_norm, need_ctx):
    a = -jnp.exp(a_log.astype(jnp.float32))
    fl = lambda arr: None if arr is None else jnp.flip(arr, axis=1)

    def prep(p, with_c):
        chans = [p['ssm_x'], p['ssm_B']] + ([p['ssm_C']] if with_c else [])
        xbc = jnp.concatenate(chans, axis=-1)
        nch = xbc.shape[-1]
        xbc = jax.nn.silu(dwconv_centred(xbc, conv_w[:, :nch], conv_b[:nch]))
        b, t, _ = xbc.shape
        xs = xbc[..., :SSM_INNER].reshape(b, t, SSM_HEADS, SSM_HEAD_DIM)
        bm = xbc[..., SSM_INNER:SSM_INNER + SSM_BC].reshape(b, t, SSM_GROUPS, SSM_STATE)
        cm = xbc[..., SSM_INNER + SSM_BC:].reshape(b, t, SSM_GROUPS, SSM_STATE) if with_c else None
        dt = jax.nn.softplus(p['ssm_dt'].astype(jnp.float32) + dt_bias.reshape(-1).astype(jnp.float32))
        return xs, bm, cm, dt[..., :SSM_HEADS], dt[..., SSM_HEADS:]

    def combine(xs, y_f, y_b, z):
        b, t = xs.shape[:2]
        y = (y_f + y_b + d_skip[:, None] * xs).reshape(b, t, SSM_INNER)
        return rmsnorm(y * jax.nn.silu(z), g_norm)

    xc, bc, cc, dtc_f, dtc_b = prep(pc, need_ctx)
    h0 = jnp.zeros((xc.shape[0], SSM_HEADS, SSM_HEAD_DIM, SSM_STATE), jnp.float32)
    yc_f, hc_f = ssd(xc, dtc_f, a[0], bc, cc, h0)
    yc_b, hc_b = ssd(fl(xc), fl(dtc_b), a[1], fl(bc), fl(cc), h0)
    xs, bm, cm, dt_f, dt_b = prep(pl, True)
    y_f, _ = ssd(xs, dt_f, a[0], bm, cm, hc_f)
    y_b, _ = ssd(fl(xs), fl(dt_b), a[1], fl(bm), fl(cm), hc_b)
    y = combine(xs, y_f, fl(y_b), pl['ssm_z'])
    yc = combine(xc, yc_f, fl(yc_b), pc['ssm_z']) if need_ctx else None
    return y, yc


def merge_branches(ys, gate_logits, w_branch, w_out):
    b, t, _ = gate_logits.shape
    g = jax.nn.sigmoid(gate_logits).reshape(b, t, N_BRANCH, D_MODEL)
    proj = jnp.einsum('btke,ked->btkd', jnp.stack(ys, axis=2), w_branch)
    return jnp.sum(g * proj, axis=2) @ w_out


def swiglu(h, w1, w3, w2):
    return (jax.nn.silu(h @ w1) * (h @ w3)) @ w2


def layer(x, xc, c, c_ctx, w_ada, b_ada, g_pre1, g_post1, g_pre2, g_post2, w_in, na_rpb,
          mla_g_q, mla_g_kv, mla_w_uq, mla_w_ukv, gqa_g_q, gqa_g_k, ssm_conv_w, ssm_conv_b,
          ssm_a_log, ssm_dt_bias, ssm_d, ssm_g_norm, w_branch, w_out, ffn_w1, ffn_w3, ffn_w2,
          rope_mla, rope_gqa, need_ctx):
    d = D_MODEL
    sh1, sc1, gt1, sh2, sc2, gt2 = jnp.split((jax.nn.silu(c) @ w_ada + b_ada)[:, None, :], 6, axis=-1)
    n_mod = 6 if need_ctx else 2
    mc = jnp.split(jax.nn.silu(c_ctx) @ w_ada[:, :n_mod * d] + b_ada[:n_mod * d], n_mod, axis=-1)
    h = rmsnorm(x, g_pre1) * (1 + sc1) + sh1
    hc = rmsnorm(xc, g_pre1) * (1 + mc[1]) + mc[0]
    p = h @ w_in
    pc = hc @ (w_in if need_ctx else w_in[:, :KEY_COLS])
    pl_, pc_ = split_cols(p, True), split_cols(pc, need_ctx)
    y_na, yc_na = na_mixer(pl_, pc_, na_rpb, need_ctx)
    y_mla, yc_mla = mla_mixer(pl_, pc_, mla_g_q, mla_g_kv, mla_w_uq, mla_w_ukv, rope_mla, need_ctx)
    y_gqa, yc_gqa = gqa_mixer(pl_, pc_, gqa_g_q, gqa_g_k, rope_gqa, need_ctx)
    y_ssm, yc_ssm = ssm_mixer(pl_, pc_, ssm_conv_w, ssm_conv_b, ssm_a_log, ssm_dt_bias, ssm_d, ssm_g_norm, need_ctx)
    mix = merge_branches([y_na, y_mla, y_gqa, y_ssm], p[..., MIX_COLS:], w_branch, w_out)
    x = x + gt1 * rmsnorm(mix, g_post1)
    h2 = rmsnorm(x, g_pre2) * (1 + sc2) + sh2
    x = x + gt2 * rmsnorm(swiglu(h2, ffn_w1, ffn_w3, ffn_w2), g_post2)
    if need_ctx:
        mix_c = merge_branches([yc_na, yc_mla, yc_gqa, yc_ssm], pc[..., MIX_COLS:], w_branch, w_out)
        xc = xc + mc[2] * rmsnorm(mix_c, g_post1)
        hc2 = rmsnorm(xc, g_pre2) * (1 + mc[4]) + mc[3]
        xc = xc + mc[5] * rmsnorm(swiglu(hc2, ffn_w1, ffn_w3, ffn_w2), g_post2)
    return x, xc


def setup_inputs(seed: int = 0) -> dict:
    key = jax.random.key(seed)
    ks = iter(jax.random.split(key, 40))
    f32 = jnp.float32

    def nrm(shape, s):
        return jax.random.normal(next(ks), shape, f32) * s

    L, D = DEPTH, D_MODEL
    x = nrm((BATCH, SEQ, D), 1.0)
    c = nrm((BATCH, D), 1.0)
    ctx = nrm((BATCH, CTX_LEN, D), 1.0)
    c_ctx = nrm((D,), 1.0)
    w_ada = nrm((L, D, 6 * D), 0.5 * D ** -0.5)
    b_ada = nrm((L, 6 * D), 0.02)
    g_pre1 = 1.0 + nrm((L, D), 0.05)
    g_post1 = 1.0 + nrm((L, D), 0.05)
    g_pre2 = 1.0 + nrm((L, D), 0.05)
    g_post2 = 1.0 + nrm((L, D), 0.05)
    w_in = nrm((L, D, IN_COLS), D ** -0.5)
    na_rpb = nrm((L, NA_HEADS, 2 * NA_ROW_WIN - 1, 2 * NA_COL_WIN - 1), 0.1)
    mla_g_q = 1.0 + nrm((L, MLA_Q_RANK), 0.05)
    mla_g_kv = 1.0 + nrm((L, MLA_KV_RANK), 0.05)
    mla_w_uq = nrm((L, MLA_Q_RANK, MLA_HEADS * (MLA_NOPE + MLA_ROPE)), MLA_Q_RANK ** -0.5)
    mla_w_ukv = nrm((L, MLA_KV_RANK, MLA_HEADS * (MLA_NOPE + MLA_V)), MLA_KV_RANK ** -0.5)
    gqa_g_q = 1.0 + nrm((L, GQA_HEAD_DIM), 0.05)
    gqa_g_k = 1.0 + nrm((L, GQA_HEAD_DIM), 0.05)
    ssm_conv_w = nrm((L, SSM_CONV, SSM_CONV_CH), SSM_CONV ** -0.5)
    ssm_conv_b = nrm((L, SSM_CONV_CH), 0.02)
    ssm_a_log = jnp.log(jax.random.uniform(next(ks), (L, 2, SSM_HEADS), f32, minval=1.0, maxval=16.0))
    dt0 = jnp.exp(jax.random.uniform(next(ks), (L, 2, SSM_HEADS), f32, minval=math.log(1e-3), maxval=math.log(1e-1)))
    ssm_dt_bias = dt0 + jnp.log(-jnp.expm1(-dt0))
    ssm_d = 1.0 + nrm((L, SSM_HEADS), 0.1)
    ssm_g_norm = 1.0 + nrm((L, SSM_INNER), 0.05)
    w_branch = nrm((L, N_BRANCH, BRANCH_W, D), BRANCH_W ** -0.5)
    w_out = nrm((L, D, D), D ** -0.5)
    ffn_w1 = nrm((L, D, FFN_HIDDEN), D ** -0.5)
    ffn_w3 = nrm((L, D, FFN_HIDDEN), D ** -0.5)
    ffn_w2 = nrm((L, FFN_HIDDEN, D), FFN_HIDDEN ** -0.5)
    return {"x": x, "c": c, "ctx": ctx, "c_ctx": c_ctx, "w_ada": w_ada, "b_ada": b_ada,
            "g_pre1": g_pre1, "g_post1": g_post1, "g_pre2": g_pre2, "g_post2": g_post2,
            "w_in": w_in, "na_rpb": na_rpb, "mla_g_q": mla_g_q, "mla_g_kv": mla_g_kv,
            "mla_w_uq": mla_w_uq, "mla_w_ukv": mla_w_ukv, "gqa_g_q": gqa_g_q, "gqa_g_k": gqa_g_k,
            "ssm_conv_w": ssm_conv_w, "ssm_conv_b": ssm_conv_b, "ssm_a_log": ssm_a_log,
            "ssm_dt_bias": ssm_dt_bias, "ssm_d": ssm_d, "ssm_g_norm": ssm_g_norm,
            "w_branch": w_branch, "w_out": w_out, "ffn_w1": ffn_w1, "ffn_w3": ffn_w3, "ffn_w2": ffn_w2}


def reference(x, c, ctx, c_ctx, w_ada, b_ada, g_pre1, g_post1, g_pre2, g_post2, w_in, na_rpb,
              mla_g_q, mla_g_kv, mla_w_uq, mla_w_ukv, gqa_g_q, gqa_g_k, ssm_conv_w, ssm_conv_b,
              ssm_a_log, ssm_dt_bias, ssm_d, ssm_g_norm, w_branch, w_out, ffn_w1, ffn_w3, ffn_w2):
    s = x.shape[1]
    rope_mla = rope_tables(s, MLA_ROPE)
    rope_gqa = rope_tables(s, GQA_HEAD_DIM)
    xc = ctx
    for l in range(DEPTH):
        x, xc = layer(x, xc, c, c_ctx, w_ada[l], b_ada[l], g_pre1[l], g_post1[l], g_pre2[l], g_post2[l],
                      w_in[l], na_rpb[l], mla_g_q[l], mla_g_kv[l], mla_w_uq[l], mla_w_ukv[l],
                      gqa_g_q[l], gqa_g_k[l], ssm_conv_w[l], ssm_conv_b[l], ssm_a_log[l], ssm_dt_bias[l],
                      ssm_d[l], ssm_g_norm[l], w_branch[l], w_out[l], ffn_w1[l], ffn_w3[l], ffn_w2[l],
                      rope_mla, rope_gqa, l < DEPTH - 1)
    return x
```

```python
import functools

import numpy as np
import jax
import jax.numpy as jnp
from jax import lax
from jax.experimental import pallas as pl
from jax.experimental.pallas import tpu as pltpu

F32 = jnp.float32
BF16 = jnp.bfloat16
HIGHEST = lax.Precision.HIGHEST

D_MODEL = 1024
SEQ = 2048
GRID_W = 64
GRID_ROWS = SEQ // GRID_W
CTX_LEN = 256
N_BRANCH = 4
BRANCH_W = 512
ROPE_THETA = 10000.0
EPS = 1e-6
NA_HEADS = 8
NA_HEAD_DIM = 64
NA_ROW_WIN = 8
NA_COL_WIN = 16
MLA_HEADS = 8
MLA_Q_RANK = 384
MLA_KV_RANK = 256
MLA_NOPE = 64
MLA_ROPE = 32
MLA_V = 64
GQA_Q_HEADS = 8
GQA_KV_HEADS = 2
GQA_HEAD_DIM = 64
SSM_HEADS = 8
SSM_HEAD_DIM = 64
SSM_GROUPS = 2
SSM_STATE = 128
SSM_CONV = 5
SSM_CHUNK = 128
SSM_INNER = SSM_HEADS * SSM_HEAD_DIM
SSM_BC = SSM_GROUPS * SSM_STATE
FFN_HIDDEN = ((8 * D_MODEL + 3 * 256 - 1) // (3 * 256)) * 256

KEY_SPLITS = (('na_k', 512), ('na_v', 512), ('mla_ckv', MLA_KV_RANK), ('mla_kr', MLA_ROPE),
              ('gqa_k', 128), ('gqa_v', 128), ('ssm_x', SSM_INNER), ('ssm_B', SSM_BC), ('ssm_dt', 2 * SSM_HEADS))
QUERY_SPLITS = (('na_q', 512), ('mla_cq', MLA_Q_RANK), ('gqa_q', 512), ('ssm_C', SSM_BC), ('ssm_z', SSM_INNER))
MIX_COLS = sum(w for _, w in KEY_SPLITS + QUERY_SPLITS)

LANES = 128
HALF = 64
N_PAIRS = 4
NEG = -1e30
VMEM_LIMIT = 52 * 1024 * 1024


def _params(n_axes):
    return pltpu.CompilerParams(dimension_semantics=("arbitrary",) * n_axes, vmem_limit_bytes=VMEM_LIMIT)


def _resident(shape):
    nd = len(shape)
    return pl.BlockSpec(shape, lambda *_: (0,) * nd, pipeline_mode=pl.Buffered(1))


def _dot(a, b):
    return jnp.dot(a, b, preferred_element_type=F32)


def _dot_nt(a, b):
    return lax.dot_general(a, b, (((1,), (1,)), ((), ())), preferred_element_type=F32)


def _silu(x):
    return x * jax.nn.sigmoid(x)


def _rms(x, g):
    return x * lax.rsqrt(jnp.mean(x * x, axis=-1, keepdims=True) + EPS) * g


def _lane(shape=(1, LANES)):
    return lax.broadcasted_iota(jnp.int32, shape, len(shape) - 1)


def _ada_kernel(c_ref, w_ref, b_ref, o_ref):
    a = _silu(c_ref[...]).astype(BF16)
    o_ref[...] = _dot(a, w_ref[...].astype(BF16)) + b_ref[...]


def _ada(c_rows, w_ada, b_ada):
    r, d = c_rows.shape
    n = w_ada.shape[1]
    tn = 1536
    return pl.pallas_call(
        _ada_kernel,
        grid=(n // tn,),
        in_specs=[pl.BlockSpec((r, d), lambda j: (0, 0)),
                  pl.BlockSpec((d, tn), lambda j: (0, j)),
                  pl.BlockSpec((1, tn), lambda j: (0, j))],
        out_specs=pl.BlockSpec((r, tn), lambda j: (0, j)),
        out_shape=jax.ShapeDtypeStruct((r, n), F32),
        compiler_params=_params(1),
        name="ada",
    )(c_rows, w_ada, b_ada.reshape(1, n))


def _in_proj_kernel(*refs, n_groups):
    x_ref, g_ref, sc_ref, sh_ref = refs[:4]
    w_refs = refs[4:4 + n_groups]
    o_refs = refs[4 + n_groups:]
    h = _rms(x_ref[...], g_ref[...]) * (1.0 + sc_ref[0]) + sh_ref[0]
    hb = h.astype(BF16)
    for w_ref, o_ref in zip(w_refs, o_refs):
        width = w_ref.shape[1]
        for c0 in range(0, width, 512):
            cw = min(512, width - c0)
            o_ref[:, c0:c0 + cw] = _dot(hb, w_ref[:, c0:c0 + cw]).astype(o_ref.dtype)


def _in_proj(x2, g, sc, sh, tiles_per_mod, weights, out_dtypes, tm=512):
    m, d = x2.shape
    mod_map = (lambda i: (0, 0, 0)) if tiles_per_mod is None else (lambda i: (i // tiles_per_mod, 0, 0))
    in_specs = [pl.BlockSpec((tm, d), lambda i: (i, 0)), _resident((1, d)),
                pl.BlockSpec((1, 1, d), mod_map), pl.BlockSpec((1, 1, d), mod_map)]
    in_specs += [_resident(w.shape) for w in weights]
    out_specs = [pl.BlockSpec((tm, w.shape[1]), lambda i: (i, 0)) for w in weights]
    out_shape = [jax.ShapeDtypeStruct((m, w.shape[1]), dt) for w, dt in zip(weights, out_dtypes)]
    return pl.pallas_call(
        functools.partial(_in_proj_kernel, n_groups=len(weights)),
        grid=(m // tm,),
        in_specs=in_specs, out_specs=out_specs, out_shape=out_shape,
        compiler_params=_params(1),
        name="in_proj",
    )(x2, g.reshape(1, d), sc, sh, *weights)


def _na_kernel(q_ref, k_ref, v_ref, kc_ref, vc_ref, bias_ref, o_ref):
    lane = _lane()
    lo = lane < HALF
    kc = kc_ref[...]
    vc = vc_ref[...]
    scale = NA_HEAD_DIM ** -0.5
    win = NA_ROW_WIN * GRID_W

    def one_row(r, carry):
        rs = jnp.clip(r - NA_ROW_WIN // 2, 0, GRID_ROWS - NA_ROW_WIN)
        d = r - rs
        q = q_ref[pl.ds(pl.multiple_of(r * GRID_W, GRID_W), GRID_W), :]
        kw = k_ref[pl.ds(pl.multiple_of(rs * GRID_W, GRID_W), win), :]
        vw = v_ref[pl.ds(pl.multiple_of(rs * GRID_W, GRID_W), win), :]
        outs = []
        for a in range(2):
            qa = jnp.where(lo if a == 0 else jnp.logical_not(lo), q, jnp.zeros_like(q))
            s_loc = _dot_nt(qa, kw) * scale + bias_ref[a, d]
            s_ctx = _dot_nt(qa, kc) * scale
            m = jnp.maximum(jnp.max(s_loc, axis=-1, keepdims=True), jnp.max(s_ctx, axis=-1, keepdims=True))
            p_loc = jnp.exp(s_loc - m)
            p_ctx = jnp.exp(s_ctx - m)
            den = jnp.sum(p_loc, axis=-1, keepdims=True) + jnp.sum(p_ctx, axis=-1, keepdims=True)
            o = _dot(p_loc.astype(BF16), vw) + _dot(p_ctx.astype(BF16), vc)
            outs.append(o / den)
        o_ref[pl.ds(pl.multiple_of(r * GRID_W, GRID_W), GRID_W), :] = jnp.where(lo, outs[0], outs[1]).astype(o_ref.dtype)
        return carry

    lax.fori_loop(0, GRID_ROWS, one_row, 0)


def _na_bias_table(rpb):
    w = np.arange(GRID_W)
    j = np.arange(GRID_W)
    cs = np.clip(w - NA_COL_WIN // 2, 0, GRID_W - NA_COL_WIN)
    valid = (j[None, :] >= cs[:, None]) & (j[None, :] < cs[:, None] + NA_COL_WIN)
    col = np.clip(j[None, :] - w[:, None] + NA_COL_WIN - 1, 0, 2 * NA_COL_WIN - 2)
    rowi = np.arange(NA_ROW_WIN)[None, :] - np.arange(NA_ROW_WIN)[:, None] + NA_ROW_WIN - 1
    t = rpb.astype(F32)[:, rowi][:, :, :, col]
    t = jnp.where(valid[None, None, None], t, NEG)
    return t.transpose(0, 1, 3, 2, 4).reshape(NA_HEADS, NA_ROW_WIN, GRID_W, NA_ROW_WIN * GRID_W)


def _na_attention(o_na_l, o_na_c, bias, nb):
    return pl.pallas_call(
        _na_kernel,
        grid=(nb, N_PAIRS),
        in_specs=[pl.BlockSpec((SEQ, LANES), lambda b, p: (b, p)),
                  pl.BlockSpec((SEQ, LANES), lambda b, p: (b, N_PAIRS + p)),
                  pl.BlockSpec((SEQ, LANES), lambda b, p: (b, 2 * N_PAIRS + p)),
                  pl.BlockSpec((CTX_LEN, LANES), lambda b, p: (b, N_PAIRS + p)),
                  pl.BlockSpec((CTX_LEN, LANES), lambda b, p: (b, 2 * N_PAIRS + p)),
                  pl.BlockSpec((2, NA_ROW_WIN, GRID_W, NA_ROW_WIN * GRID_W), lambda b, p: (p, 0, 0, 0))],
        out_specs=pl.BlockSpec((SEQ, LANES), lambda b, p: (b, p)),
        out_shape=jax.ShapeDtypeStruct((nb * SEQ, NA_HEADS * NA_HEAD_DIM), BF16),
        compiler_params=_params(2),
        name="na_attn",
    )(o_na_l, o_na_l, o_na_l, o_na_c, o_na_c, bias)


def _attn_kernel(*refs, n_src, dq, scale):
    q_ref = refs[0]
    o_ref = refs[-1]
    lane = _lane()
    lo = lane < HALF
    q = q_ref[...]
    if scale != 1.0:
        q = (q.astype(F32) * scale).astype(q.dtype)
    outs = []
    for a in range(2):
        if dq == LANES:
            qa = q[:, a * LANES:(a + 1) * LANES]
        else:
            qa = jnp.where(lo if a == 0 else jnp.logical_not(lo), q, jnp.zeros_like(q))
        ss = []
        for s in range(n_src):
            k = refs[1 + 2 * s][...]
            ka = k[:, a * LANES:(a + 1) * LANES] if dq == LANES else k
            ss.append(_dot_nt(qa, ka))
        m = ss[0].max(axis=-1, keepdims=True)
        for sc in ss[1:]:
            m = jnp.maximum(m, sc.max(axis=-1, keepdims=True))
        den = None
        o = None
        for s in range(n_src):
            p = jnp.exp(ss[s] - m)
            ps = jnp.sum(p, axis=-1, keepdims=True)
            po = _dot(p.astype(BF16), refs[2 + 2 * s][...])
            den = ps if den is None else den + ps
            o = po if o is None else o + po
        outs.append(o / den)
    o_ref[...] = jnp.where(lo, outs[0], outs[1]).astype(o_ref.dtype)


def _attention(q, srcs, *, nb, t, dq, tq, scale=1.0, q_off=0, k_map=lambda p: p, v_map=lambda p: p):
    nq = t // tq
    in_specs = [pl.BlockSpec((tq, 2 * dq), lambda b, p, i: (b * nq + i, q_off + p))]
    args = [q]
    for k_arr, v_arr, u, k_off, v_off in srcs:
        in_specs.append(pl.BlockSpec((u, 2 * dq), lambda b, p, i, k_off=k_off: (b, k_off + k_map(p))))
        in_specs.append(pl.BlockSpec((u, LANES), lambda b, p, i, v_off=v_off: (b, v_off + v_map(p))))
        args += [k_arr, v_arr]
    return pl.pallas_call(
        functools.partial(_attn_kernel, n_src=len(srcs), dq=dq, scale=scale),
        grid=(nb, N_PAIRS, nq),
        in_specs=in_specs,
        out_specs=pl.BlockSpec((tq, LANES), lambda b, p, i: (b * nq + i, p)),
        out_shape=jax.ShapeDtypeStruct((nb * t, N_PAIRS * LANES), BF16),
        compiler_params=_params(3),
        name="attn",
    )(*args)


def _mla_prep_kernel(*refs, rotate, scale):
    if rotate:
        x_ref, gq_ref, gkv_ref, wq_ref, wk_ref, wv_ref, cos_ref, sin_ref, q_ref, k_ref, v_ref = refs
    else:
        x_ref, gq_ref, gkv_ref, wq_ref, wk_ref, wv_ref, q_ref, k_ref, v_ref = refs
    x = x_ref[...].astype(F32)
    cq = x[:, :MLA_Q_RANK]
    ckv = x[:, MLA_Q_RANK:MLA_Q_RANK + MLA_KV_RANK]
    kr = x[:, MLA_Q_RANK + MLA_KV_RANK:]
    cqn = _rms(cq, gq_ref[...]).astype(BF16)
    ckvn = _rms(ckv, gkv_ref[...]).astype(BF16)
    lane = _lane()
    first_half = lane < MLA_NOPE + MLA_ROPE // 2

    def rope(t):
        if not rotate:
            return t
        rot = jnp.where(first_half, pltpu.roll(t, LANES - MLA_ROPE // 2, axis=1), pltpu.roll(t, MLA_ROPE // 2, axis=1))
        return t * cos_ref[...] + rot * sin_ref[...]

    kr = rope(kr)
    for h in range(MLA_HEADS):
        sl = slice(h * LANES, (h + 1) * LANES)
        qh = rope(_dot(cqn, wq_ref[:, sl]))
        q_ref[:, sl] = (qh * scale).astype(q_ref.dtype)
        k_ref[:, sl] = (_dot(ckvn, wk_ref[:, sl]) + kr).astype(k_ref.dtype)
    v_ref[...] = _dot(ckvn, wv_ref[...]).astype(v_ref.dtype)


def _mla_prep(o_mla, gq, gkv, wq, wk, wv, tables, tm=512):
    m = o_mla.shape[0]
    rotate = tables is not None
    scale = (MLA_NOPE + MLA_ROPE) ** -0.5
    in_specs = [pl.BlockSpec((tm, o_mla.shape[1]), lambda i: (i, 0)),
                _resident((1, MLA_Q_RANK)), _resident((1, MLA_KV_RANK)),
                _resident(wq.shape), _resident(wk.shape), _resident(wv.shape)]
    args = [o_mla, gq.reshape(1, -1), gkv.reshape(1, -1), wq, wk, wv]
    if rotate:
        nt = SEQ // tm
        in_specs += [pl.BlockSpec((tm, LANES), lambda i: (i % nt, 0))] * 2
        args += list(tables)
    widths = (MLA_HEADS * LANES, MLA_HEADS * LANES, MLA_HEADS * MLA_V)
    return pl.pallas_call(
        functools.partial(_mla_prep_kernel, rotate=rotate, scale=scale),
        grid=(m // tm,),
        in_specs=in_specs,
        out_specs=[pl.BlockSpec((tm, w), lambda i: (i, 0)) for w in widths],
        out_shape=[jax.ShapeDtypeStruct((m, w), BF16) for w in widths],
        compiler_params=_params(1),
        name="mla_prep",
    )(*args)


def _gqa_prep_kernel(*refs, rotate, scale):
    if rotate:
        x_ref, gq_ref, gk_ref, bd_ref, cos_ref, sin_ref, q_ref, k_ref, v_ref = refs
    else:
        x_ref, gq_ref, gk_ref, bd_ref, q_ref, k_ref, v_ref = refs
    x = x_ref[...].astype(F32)
    lane = _lane()
    lo = lane < HALF
    first_half = (lane & (GQA_HEAD_DIM - 1)) < GQA_HEAD_DIM // 2

    def headnorm(t, g):
        ss = _dot((t * t).astype(BF16), bd_ref[...]) * (1.0 / GQA_HEAD_DIM)
        return t * lax.rsqrt(ss + EPS) * g

    def rope(t):
        if not rotate:
            return t
        rot = jnp.where(first_half, pltpu.roll(t, LANES - GQA_HEAD_DIM // 2, axis=1), pltpu.roll(t, GQA_HEAD_DIM // 2, axis=1))
        return t * cos_ref[...] + rot * sin_ref[...]

    for j in range(N_PAIRS):
        sl = slice(j * LANES, (j + 1) * LANES)
        q_ref[:, sl] = (rope(headnorm(x[:, sl], gq_ref[...])) * scale).astype(q_ref.dtype)
    nq = GQA_Q_HEADS * GQA_HEAD_DIM
    k = rope(headnorm(x[:, nq:nq + LANES], gk_ref[...]))
    v = x[:, nq + LANES:nq + 2 * LANES]
    for t, ref in ((k, k_ref), (v, v_ref)):
        sw = pltpu.roll(t, HALF, axis=1)
        ref[:, :LANES] = jnp.where(lo, t, sw).astype(ref.dtype)
        ref[:, LANES:] = jnp.where(lo, sw, t).astype(ref.dtype)


def _gqa_prep(o_gqa, gq, gk, tables, tm=512):
    m = o_gqa.shape[0]
    rotate = tables is not None
    bd = jnp.asarray(np.kron(np.eye(2), np.ones((HALF, HALF))), BF16)
    in_specs = [pl.BlockSpec((tm, o_gqa.shape[1]), lambda i: (i, 0)),
                _resident((1, LANES)), _resident((1, LANES)), _resident((LANES, LANES))]
    args = [o_gqa, jnp.tile(gq, 2).reshape(1, LANES), jnp.tile(gk, 2).reshape(1, LANES), bd]
    if rotate:
        nt = SEQ // tm
        in_specs += [pl.BlockSpec((tm, LANES), lambda i: (i % nt, 0))] * 2
        args += list(tables)
    widths = (GQA_Q_HEADS * GQA_HEAD_DIM, 2 * LANES, 2 * LANES)
    return pl.pallas_call(
        functools.partial(_gqa_prep_kernel, rotate=rotate, scale=GQA_HEAD_DIM ** -0.5),
        grid=(m // tm,),
        in_specs=in_specs,
        out_specs=[pl.BlockSpec((tm, w), lambda i: (i, 0)) for w in widths],
        out_shape=[jax.ShapeDtypeStruct((m, w), BF16) for w in widths],
        compiler_params=_params(1),
        name="gqa_prep",
    )(*args)


def _rope_tables(dim, lane_start, group):
    t = jnp.arange(SEQ)
    row = (t // GRID_W).astype(F32)
    col = (t % GRID_W).astype(F32)
    quarter = dim // 4
    inv = ROPE_THETA ** (-jnp.arange(quarter, dtype=F32) / quarter)
    ang = jnp.concatenate([row[:, None] * inv, col[:, None] * inv], axis=-1)
    cos, sin = jnp.cos(ang), jnp.sin(ang)
    cos_g = jnp.ones((SEQ, group), F32).at[:, lane_start:lane_start + dim].set(jnp.concatenate([cos, cos], axis=-1))
    sin_g = jnp.zeros((SEQ, group), F32).at[:, lane_start:lane_start + dim].set(jnp.concatenate([-sin, sin], axis=-1))
    reps = LANES // group
    return jnp.tile(cos_g, (1, reps)), jnp.tile(sin_g, (1, reps))


CONV_HALO = 16


def _conv_kernel(x_ref, prev_ref, next_ref, w_ref, b_ref, o_ref):
    i = pl.program_id(1)
    n = pl.num_programs(1)
    tm = x_ref.shape[0]
    rows = tm + 2 * CONV_HALO
    for c0 in range(0, x_ref.shape[1], LANES):
        sl = slice(c0, c0 + LANES)
        prev = jnp.where(i > 0, prev_ref[:, sl].astype(F32), 0.0)
        nxt = jnp.where(i < n - 1, next_ref[:, sl].astype(F32), 0.0)
        xcat = jnp.concatenate([prev, x_ref[:, sl].astype(F32), nxt], axis=0)
        acc = jnp.zeros((tm, LANES), F32) + b_ref[:, sl]
        for k in range(SSM_CONV):
            sh = (SSM_CONV // 2 - k) % rows
            rolled = xcat if sh == 0 else pltpu.roll(xcat, sh, axis=0)
            acc = acc + rolled[CONV_HALO:CONV_HALO + tm] * w_ref[k:k + 1, sl]
        o_ref[:, sl] = _silu(acc).astype(o_ref.dtype)


def _ssm_conv(o_ssm, conv_w, conv_b, nb, t, tm=256):
    nch = conv_w.shape[1]
    nt = t // tm
    hb = tm // CONV_HALO
    total = nb * t // CONV_HALO
    w8 = jnp.zeros((8, nch), F32).at[:SSM_CONV].set(conv_w)
    return pl.pallas_call(
        _conv_kernel,
        grid=(nb, nt),
        in_specs=[pl.BlockSpec((tm, nch), lambda b, i: (b * nt + i, 0)),
                  pl.BlockSpec((CONV_HALO, nch), lambda b, i: (jnp.maximum((b * nt + i) * hb - 1, 0), 0)),
                  pl.BlockSpec((CONV_HALO, nch), lambda b, i: (jnp.minimum((b * nt + i + 1) * hb, total - 1), 0)),
                  _resident((8, nch)), _resident((1, nch))],
        out_specs=pl.BlockSpec((tm, nch), lambda b, i: (b * nt + i, 0)),
        out_shape=jax.ShapeDtypeStruct((nb * t, nch), BF16),
        compiler_params=_params(2),
        name="ssm_conv",
    )(o_ssm, o_ssm, o_ssm, w8, conv_b.reshape(1, nch))


def _softplus(x):
    return jnp.maximum(x, 0.0) + jnp.log(1.0 + jnp.exp(-jnp.abs(x)))


def _ssd_kernel(*refs, with_y):
    if with_y:
        x_ref, b_ref, c_ref, bt_ref, dt_ref, dtt_ref, al_ref, alt_ref, bi_ref, bit_ref, h0_ref, y_ref, hf_ref, st = refs
    else:
        x_ref, b_ref, bt_ref, dt_ref, dtt_ref, al_ref, alt_ref, bi_ref, bit_ref, h0_ref, hf_ref, st = refs
    d = pl.program_id(0)
    j = pl.program_id(2)
    q = SSM_CHUNK

    @pl.when(j == 0)
    def _():
        st[...] = h0_ref[0, 0]

    row = lax.broadcasted_iota(jnp.int32, (q, q), 0)
    col = lax.broadcasted_iota(jnp.int32, (q, q), 1)
    dsg = (col - row) * (1 - 2 * d)
    m_qs = dsg <= 0
    m_ts = dsg >= 0
    lane = _lane()
    lo = lane < HALF

    def col_of(x, h):
        return jnp.sum(jnp.where(lane == h, x, 0.0), axis=1, keepdims=True)

    dt = _softplus(dt_ref[0] + bi_ref[0])
    da = jnp.where(lane < SSM_HEADS, dt * (-jnp.exp(al_ref[0])), 0.0)
    a_cum_c = jnp.dot(jnp.where(m_qs, 1.0, 0.0), da, precision=HIGHEST, preferred_element_type=F32)
    a_tot = jnp.sum(da, axis=0, keepdims=True)
    dat = _softplus(dtt_ref[0, 0] + bit_ref[0]) * (-jnp.exp(alt_ref[0]))
    a_cum_r = jnp.dot(dat, jnp.where(m_ts, 1.0, 0.0), precision=HIGHEST, preferred_element_type=F32)

    x = x_ref[...].astype(F32)
    bm = b_ref[...]
    bt = bt_ref[0]
    if with_y:
        cm = c_ref[...]
    for g in range(SSM_GROUPS):
        btg = bt[g * SSM_STATE:(g + 1) * SSM_STATE, :]
        if with_y:
            cg = cm[:, g * SSM_STATE:(g + 1) * SSM_STATE]
            cb = _dot(cg, btg)
        for pp in range(2):
            p = 2 * g + pp
            h0, h1 = 2 * p, 2 * p + 1
            ac0, ac1 = col_of(a_cum_c, h0), col_of(a_cum_c, h1)
            tot0, tot1 = col_of(a_tot, h0), col_of(a_tot, h1)
            xdt = x[:, p * LANES:(p + 1) * LANES] * jnp.where(lo, col_of(dt, h0), col_of(dt, h1))
            ht = st[p]
            if with_y:
                xdt_b = xdt.astype(BF16)
                l0 = jnp.exp(jnp.where(m_qs, ac0 - a_cum_r[h0:h0 + 1, :], NEG))
                l1 = jnp.exp(jnp.where(m_qs, ac1 - a_cum_r[h1:h1 + 1, :], NEG))
                y_in = jnp.where(lo, _dot((cb * l0).astype(BF16), xdt_b), _dot((cb * l1).astype(BF16), xdt_b))
                y_st = _dot(cg, ht.astype(BF16)) * jnp.where(lo, jnp.exp(ac0), jnp.exp(ac1))
                y_ref[0, :, p * LANES:(p + 1) * LANES] = (y_in + y_st).astype(y_ref.dtype)
            dec = jnp.where(lo, jnp.exp(tot0 - ac0), jnp.exp(tot1 - ac1))
            st[p] = ht * jnp.where(lo, jnp.exp(tot0), jnp.exp(tot1)) + _dot(btg, (xdt * dec).astype(BF16))

    @pl.when(j == pl.num_programs(2) - 1)
    def _():
        hf_ref[0, 0] = st[...]


def _ssd(xbc, b_t, dt2, dtt2, a_log, dt_bias, h0, *, nb, t, with_y):
    nc = t // SSM_CHUNK
    nch = xbc.shape[1]

    def cidx(d, j):
        return j + d * (nc - 1 - 2 * j)

    pad = LANES - SSM_HEADS
    al_row = jnp.pad(a_log.astype(F32), ((0, 0), (0, pad))).reshape(2, 1, LANES)
    bi_row = jnp.pad(dt_bias.astype(F32), ((0, 0), (0, pad))).reshape(2, 1, LANES)
    al_t = jnp.broadcast_to(a_log.astype(F32)[:, :, None], (2, SSM_HEADS, SSM_CHUNK))
    bi_t = jnp.broadcast_to(dt_bias.astype(F32)[:, :, None], (2, SSM_HEADS, SSM_CHUNK))
    in_specs = [pl.BlockSpec((SSM_CHUNK, SSM_INNER), lambda d, b, j: (b * nc + cidx(d, j), 0)),
                pl.BlockSpec((SSM_CHUNK, SSM_BC), lambda d, b, j: (b * nc + cidx(d, j), SSM_INNER // SSM_BC))]
    args = [xbc, xbc]
    if with_y:
        in_specs.append(pl.BlockSpec((SSM_CHUNK, SSM_BC), lambda d, b, j: (b * nc + cidx(d, j), SSM_INNER // SSM_BC + 1)))
        args.append(xbc)
    in_specs += [pl.BlockSpec((1, SSM_BC, SSM_CHUNK), lambda d, b, j: (b, 0, cidx(d, j))),
                 pl.BlockSpec((1, SSM_CHUNK, LANES), lambda d, b, j: (d, b * nc + cidx(d, j), 0)),
                 pl.BlockSpec((1, 1, SSM_HEADS, SSM_CHUNK), lambda d, b, j: (d, b, 0, cidx(d, j))),
                 pl.BlockSpec((1, 1, LANES), lambda d, b, j: (d, 0, 0)),
                 pl.BlockSpec((1, SSM_HEADS, SSM_CHUNK), lambda d, b, j: (d, 0, 0)),
                 pl.BlockSpec((1, 1, LANES), lambda d, b, j: (d, 0, 0)),
                 pl.BlockSpec((1, SSM_HEADS, SSM_CHUNK), lambda d, b, j: (d, 0, 0)),
                 pl.BlockSpec((1, 1, N_PAIRS, SSM_STATE, LANES), lambda d, b, j: (d, b, 0, 0, 0))]
    args += [b_t, dt2, dtt2, al_row, al_t, bi_row, bi_t, h0]
    st_spec = pl.BlockSpec((1, 1, N_PAIRS, SSM_STATE, LANES), lambda d, b, j: (d, b, 0, 0, 0))
    st_shape = jax.ShapeDtypeStruct((2, nb, N_PAIRS, SSM_STATE, LANES), F32)
    if with_y:
        out_specs = [pl.BlockSpec((1, SSM_CHUNK, SSM_INNER), lambda d, b, j: (d, b * nc + cidx(d, j), 0)), st_spec]
        out_shape = [jax.ShapeDtypeStruct((2, nb * t, SSM_INNER), BF16), st_shape]
    else:
        out_specs = [st_spec]
        out_shape = [st_shape]
    return pl.pallas_call(
        functools.partial(_ssd_kernel, with_y=with_y),
        grid=(2, nb, nc),
        in_specs=in_specs, out_specs=out_specs, out_shape=out_shape,
        scratch_shapes=[pltpu.VMEM((N_PAIRS, SSM_STATE, LANES), F32)],
        compiler_params=_params(3),
        name="ssd",
    )(*args)


def _ssm_combine_kernel(yf_ref, yb_ref, x_ref, z_ref, d_ref, g_ref, o_ref):
    y = yf_ref[0].astype(F32) + yb_ref[0].astype(F32) + d_ref[...] * x_ref[...].astype(F32)
    y = y * _silu(z_ref[...].astype(F32))
    o_ref[...] = _rms(y, g_ref[...]).astype(o_ref.dtype)


def _ssm_combine(y2, xbc, o_ssm, d_skip, g_norm, tm=512):
    m = xbc.shape[0]
    dvec = jnp.repeat(d_skip.astype(F32), SSM_HEAD_DIM).reshape(1, SSM_INNER)
    z_blk = o_ssm.shape[1] // SSM_INNER - 1
    return pl.pallas_call(
        _ssm_combine_kernel,
        grid=(m // tm,),
        in_specs=[pl.BlockSpec((1, tm, SSM_INNER), lambda i: (0, i, 0)),
                  pl.BlockSpec((1, tm, SSM_INNER), lambda i: (1, i, 0)),
                  pl.BlockSpec((tm, SSM_INNER), lambda i: (i, 0)),
                  pl.BlockSpec((tm, SSM_INNER), lambda i: (i, z_blk)),
                  _resident((1, SSM_INNER)), _resident((1, SSM_INNER))],
        out_specs=pl.BlockSpec((tm, SSM_INNER), lambda i: (i, 0)),
        out_shape=jax.ShapeDtypeStruct((m, SSM_INNER), BF16),
        compiler_params=_params(1),
        name="ssm_combine",
    )(y2, y2, xbc, o_ssm, dvec, g_norm.reshape(1, SSM_INNER))


def _merge_kernel(yna_ref, ymla_ref, ygqa_ref, yssm_ref, gate_ref, wb_ref, wo_ref, x_ref, gt_ref, g_ref, o_ref):
    acc = None
    for k, y_ref in enumerate((yna_ref, ymla_ref, ygqa_ref, yssm_ref)):
        proj = _dot(y_ref[...], wb_ref[k])
        gate = jax.nn.sigmoid(gate_ref[:, k * D_MODEL:(k + 1) * D_MODEL].astype(F32))
        acc = gate * proj if acc is None else acc + gate * proj
    mix = _dot(acc.astype(BF16), wo_ref[...])
    o_ref[...] = x_ref[...] + gt_ref[0] * _rms(mix, g_ref[...])


def _merge(ys, gate, wb, wo, x2, gt, g_post, tiles_per_mod, tm=512):
    m, d = x2.shape
    mod_map = (lambda i: (0, 0, 0)) if tiles_per_mod is None else (lambda i: (i // tiles_per_mod, 0, 0))
    in_specs = [pl.BlockSpec((tm, BRANCH_W), lambda i: (i, 0))] * N_BRANCH
    in_specs += [pl.BlockSpec((tm, N_BRANCH * d), lambda i: (i, 0)),
                 _resident(wb.shape), _resident(wo.shape),
                 pl.BlockSpec((tm, d), lambda i: (i, 0)),
                 pl.BlockSpec((1, 1, d), mod_map), _resident((1, d))]
    return pl.pallas_call(
        _merge_kernel,
        grid=(m // tm,),
        in_specs=in_specs,
        out_specs=pl.BlockSpec((tm, d), lambda i: (i, 0)),
        out_shape=jax.ShapeDtypeStruct((m, d), F32),
        compiler_params=_params(1),
        name="merge",
    )(*ys, gate, wb, wo, x2, gt, g_post.reshape(1, d))


def _ffn_kernel(x_ref, gpre_ref, sc_ref, sh_ref, w1_ref, w3_ref, w2_ref, gt_ref, gpost_ref, o_ref):
    x = x_ref[...]
    hb = (_rms(x, gpre_ref[...]) * (1.0 + sc_ref[0]) + sh_ref[0]).astype(BF16)
    u = _silu(_dot(hb, w1_ref[...])) * _dot(hb, w3_ref[...])
    f = _dot(u.astype(BF16), w2_ref[...])
    o_ref[...] = x + gt_ref[0] * _rms(f, gpost_ref[...])


def _ffn(x2, g_pre, sc, sh, w1, w3, w2, gt, g_post, tiles_per_mod, tm=512):
    m, d = x2.shape
    mod_map = (lambda i: (0, 0, 0)) if tiles_per_mod is None else (lambda i: (i // tiles_per_mod, 0, 0))
    mod = pl.BlockSpec((1, 1, d), mod_map)
    return pl.pallas_call(
        _ffn_kernel,
        grid=(m // tm,),
        in_specs=[pl.BlockSpec((tm, d), lambda i: (i, 0)), _resident((1, d)), mod, mod,
                  _resident(w1.shape), _resident(w3.shape), _resident(w2.shape), mod, _resident((1, d))],
        out_specs=pl.BlockSpec((tm, d), lambda i: (i, 0)),
        out_shape=jax.ShapeDtypeStruct((m, d), F32),
        compiler_params=_params(1),
        name="ffn",
    )(x2, g_pre.reshape(1, d), sc, sh, w1, w3, w2, gt, g_post.reshape(1, d))


def _split_w_in(w_in):
    off, cols = 0, {}
    for name, w in KEY_SPLITS + QUERY_SPLITS:
        cols[name] = w_in[:, off:off + w]
        off += w
    d = w_in.shape[0]
    z = lambda n: jnp.zeros((d, n), w_in.dtype)
    cat = lambda parts: jnp.concatenate(parts, axis=1).astype(BF16)
    w_na = cat([cols['na_q'], cols['na_k'], cols['na_v']])
    w_mla = cat([cols['mla_cq'], cols['mla_ckv'], z(MLA_NOPE), cols['mla_kr'], z(LANES - MLA_NOPE - MLA_ROPE)])
    w_gqa = cat([cols['gqa_q'], cols['gqa_k'], cols['gqa_v']])
    w_ssm = cat([cols['ssm_x'], cols['ssm_B'], cols['ssm_C'], cols['ssm_z']])
    w_dt = cat([cols['ssm_dt'], z(LANES - 2 * SSM_HEADS)])
    w_gate = w_in[:, MIX_COLS:].astype(BF16)
    return w_na, w_mla, w_gqa, w_ssm, w_dt, w_gate


def _dt_layouts(o_dt, nb, t):
    both = o_dt[:, :2 * SSM_HEADS].reshape(nb * t, 2, SSM_HEADS).transpose(1, 0, 2)
    dt2 = jnp.pad(both, ((0, 0), (0, 0), (0, LANES - SSM_HEADS)))
    dtt2 = both.reshape(2, nb, t, SSM_HEADS).transpose(0, 1, 3, 2)
    return dt2, dtt2


def _layer(x2, xc2, mods, mc, lw, tables, nb, need_ctx):
    d = D_MODEL
    tpm = SEQ // 512
    chunk = lambda a, k: a[:, k * d:(k + 1) * d].reshape(-1, 1, d)
    sh1, sc1, gt1, sh2, sc2, gt2 = (chunk(mods, k) for k in range(6))
    csh1, csc1, cgt1, csh2, csc2, cgt2 = (chunk(mc, k) for k in range(6))
    w_na, w_mla, w_gqa, w_ssm, w_dt, w_gate = _split_w_in(lw['w_in'])
    dts = (BF16, BF16, BF16, BF16, F32, BF16)
    o_na, o_mla, o_gqa, o_ssm, o_dt, gate = _in_proj(
        x2, lw['g_pre1'], sc1, sh1, tpm, (w_na, w_mla, w_gqa, w_ssm, w_dt, w_gate), dts)
    if need_ctx:
        c_na, c_mla, c_gqa, c_ssm, c_dt, c_gate = _in_proj(
            xc2, lw['g_pre1'], csc1, csh1, None, (w_na, w_mla, w_gqa, w_ssm, w_dt, w_gate), dts)
    else:
        c_na, c_mla, c_gqa, c_ssm, c_dt = _in_proj(
            xc2, lw['g_pre1'], csc1, csh1, None, (w_na, w_mla, w_gqa, w_ssm, w_dt), dts[:5])

    y_na = _na_attention(o_na, c_na, _na_bias_table(lw['na_rpb']), nb)

    wq = jnp.pad(lw['mla_w_uq'].reshape(MLA_Q_RANK, MLA_HEADS, MLA_NOPE + MLA_ROPE),
                 ((0, 0), (0, 0), (0, LANES - MLA_NOPE - MLA_ROPE))).reshape(MLA_Q_RANK, MLA_HEADS * LANES).astype(BF16)
    wkv = lw['mla_w_ukv'].reshape(MLA_KV_RANK, MLA_HEADS, MLA_NOPE + MLA_V)
    wk = jnp.pad(wkv[:, :, :MLA_NOPE], ((0, 0), (0, 0), (0, LANES - MLA_NOPE))).reshape(MLA_KV_RANK, MLA_HEADS * LANES).astype(BF16)
    wv = wkv[:, :, MLA_NOPE:].reshape(MLA_KV_RANK, MLA_HEADS * MLA_V).astype(BF16)
    ql, kl, vl = _mla_prep(o_mla, lw['mla_g_q'], lw['mla_g_kv'], wq, wk, wv, tables['mla'])
    qc, kc, vc = _mla_prep(c_mla, lw['mla_g_q'], lw['mla_g_kv'], wq, wk, wv, None)
    y_mla = _attention(ql, [(kc, vc, CTX_LEN, 0, 0), (kl, vl, SEQ, 0, 0)], nb=nb, t=SEQ, dq=LANES, tq=512)

    gql, gkl, gvl = _gqa_prep(o_gqa, lw['gqa_g_q'], lw['gqa_g_k'], tables['gqa'])
    gqc, gkc, gvc = _gqa_prep(c_gqa, lw['gqa_g_q'], lw['gqa_g_k'], None)
    half_map = lambda p: p // 2
    y_gqa = _attention(gql, [(gkc, gvc, CTX_LEN, 0, 0), (gkl, gvl, SEQ, 0, 0)], nb=nb, t=SEQ, dq=HALF, tq=512,
                       k_map=half_map, v_map=half_map)

    nxbc = SSM_INNER + 2 * SSM_BC
    conv_w, conv_b = lw['ssm_conv_w'], lw['ssm_conv_b']
    xbc_l = _ssm_conv(o_ssm, conv_w, conv_b, nb, SEQ)
    xbc_c = _ssm_conv(c_ssm, conv_w, conv_b, nb, CTX_LEN)
    bt_of = lambda xbc, t: xbc[:, SSM_INNER:SSM_INNER + SSM_BC].reshape(nb, t, SSM_BC).transpose(0, 2, 1)
    h_zero = jnp.zeros((2, nb, N_PAIRS, SSM_STATE, LANES), F32)
    dt2c, dtt2c = _dt_layouts(c_dt, nb, CTX_LEN)
    res_c = _ssd(xbc_c, bt_of(xbc_c, CTX_LEN), dt2c, dtt2c, lw['ssm_a_log'], lw['ssm_dt_bias'], h_zero,
                 nb=nb, t=CTX_LEN, with_y=need_ctx)
    h_ctx = res_c[-1]
    dt2l, dtt2l = _dt_layouts(o_dt, nb, SEQ)
    y2_l, _ = _ssd(xbc_l, bt_of(xbc_l, SEQ), dt2l, dtt2l, lw['ssm_a_log'], lw['ssm_dt_bias'], h_ctx,
                   nb=nb, t=SEQ, with_y=True)
    y_ssm = _ssm_combine(y2_l, xbc_l, o_ssm, lw['ssm_d'], lw['ssm_g_norm'])

    wb = lw['w_branch'].astype(BF16)
    wo = lw['w_out'].astype(BF16)
    w1, w3, w2 = lw['ffn_w1'].astype(BF16), lw['ffn_w3'].astype(BF16), lw['ffn_w2'].astype(BF16)
    x2 = _merge((y_na, y_mla, y_gqa, y_ssm), gate, wb, wo, x2, gt1, lw['g_post1'], tpm)
    x2 = _ffn(x2, lw['g_pre2'], sc2, sh2, w1, w3, w2, gt2, lw['g_post2'], tpm)

    if need_ctx:
        yc_na = _attention(c_na, [(c_na, c_na, CTX_LEN, N_PAIRS, 2 * N_PAIRS)], nb=nb, t=CTX_LEN, dq=HALF,
                           tq=CTX_LEN, scale=NA_HEAD_DIM ** -0.5)
        yc_mla = _attention(qc, [(kc, vc, CTX_LEN, 0, 0)], nb=nb, t=CTX_LEN, dq=LANES, tq=CTX_LEN)
        yc_gqa = _attention(gqc, [(gkc, gvc, CTX_LEN, 0, 0)], nb=nb, t=CTX_LEN, dq=HALF, tq=CTX_LEN,
                            k_map=half_map, v_map=half_map)
        yc_ssm = _ssm_combine(res_c[0], xbc_c, c_ssm, lw['ssm_d'], lw['ssm_g_norm'])
        xc2 = _merge((yc_na, yc_mla, yc_gqa, yc_ssm), c_gate, wb, wo, xc2, cgt1, lw['g_post1'], None)
        xc2 = _ffn(xc2, lw['g_pre2'], csc2, csh2, w1, w3, w2, cgt2, lw['g_post2'], None)
    return x2, xc2


def kernel(x, c, ctx, c_ctx, w_ada, b_ada, g_pre1, g_post1, g_pre2, g_post2, w_in, na_rpb, mla_g_q, mla_g_kv, mla_w_uq, mla_w_ukv, gqa_g_q, gqa_g_k, ssm_conv_w, ssm_conv_b, ssm_a_log, ssm_dt_bias, ssm_d, ssm_g_norm, w_branch, w_out, ffn_w1, ffn_w3, ffn_w2):
    nb, s, d = x.shape
    assert s == SEQ and d == D_MODEL and ctx.shape[1] == CTX_LEN
    depth = w_ada.shape[0]
    tables = {'mla': _rope_tables(MLA_ROPE, MLA_NOPE, LANES), 'gqa': _rope_tables(GQA_HEAD_DIM, 0, GQA_HEAD_DIM)}
    x2 = x.reshape(nb * s, d)
    xc2 = ctx.reshape(nb * CTX_LEN, d)
    n_rows = -(-(nb + 1) // 8) * 8
    c_rows = jnp.zeros((n_rows, d), F32).at[:nb].set(c).at[nb].set(c_ctx)
    stacked = dict(w_in=w_in, na_rpb=na_rpb, mla_g_q=mla_g_q, mla_g_kv=mla_g_kv, mla_w_uq=mla_w_uq,
                   mla_w_ukv=mla_w_ukv, gqa_g_q=gqa_g_q, gqa_g_k=gqa_g_k, ssm_conv_w=ssm_conv_w,
                   ssm_conv_b=ssm_conv_b, ssm_a_log=ssm_a_log, ssm_dt_bias=ssm_dt_bias, ssm_d=ssm_d,
                   ssm_g_norm=ssm_g_norm, w_branch=w_branch, w_out=w_out, ffn_w1=ffn_w1, ffn_w3=ffn_w3,
                   ffn_w2=ffn_w2, g_pre1=g_pre1, g_post1=g_post1, g_pre2=g_pre2, g_post2=g_post2)
    for l in range(depth):
        lw = {k: v[l] for k, v in stacked.items()}
        mod_all = _ada(c_rows, w_ada[l], b_ada[l])
        x2, xc2 = _layer(x2, xc2, mod_all[:nb], mod_all[nb:nb + 1], lw, tables, nb, l < depth - 1)
    return x2.reshape(nb, s, d)
```

```python
import functools

import numpy as np
import jax
import jax.numpy as jnp
from jax import lax
from jax.experimental import pallas as pl
from jax.experimental.pallas import tpu as pltpu

F32 = jnp.float32
BF16 = jnp.bfloat16
HIGHEST = lax.Precision.HIGHEST

D_MODEL = 1024
SEQ = 2048
GRID_W = 64
GRID_ROWS = SEQ // GRID_W
CTX_LEN = 256
N_BRANCH = 4
BRANCH_W = 512
ROPE_THETA = 10000.0
EPS = 1e-6
NA_HEADS = 8
NA_HEAD_DIM = 64
NA_ROW_WIN = 8
NA_COL_WIN = 16
MLA_HEADS = 8
MLA_Q_RANK = 384
MLA_KV_RANK = 256
MLA_NOPE = 64
MLA_ROPE = 32
MLA_V = 64
GQA_Q_HEADS = 8
GQA_KV_HEADS = 2
GQA_HEAD_DIM = 64
SSM_HEADS = 8
SSM_HEAD_DIM = 64
SSM_GROUPS = 2
SSM_STATE = 128
SSM_CONV = 5
SSM_CHUNK = 128
SSM_INNER = SSM_HEADS * SSM_HEAD_DIM
SSM_BC = SSM_GROUPS * SSM_STATE
FFN_HIDDEN = ((8 * D_MODEL + 3 * 256 - 1) // (3 * 256)) * 256

KEY_SPLITS = (('na_k', 512), ('na_v', 512), ('mla_ckv', MLA_KV_RANK), ('mla_kr', MLA_ROPE),
              ('gqa_k', 128), ('gqa_v', 128), ('ssm_x', SSM_INNER), ('ssm_B', SSM_BC), ('ssm_dt', 2 * SSM_HEADS))
QUERY_SPLITS = (('na_q', 512), ('mla_cq', MLA_Q_RANK), ('gqa_q', 512), ('ssm_C', SSM_BC), ('ssm_z', SSM_INNER))
MIX_COLS = sum(w for _, w in KEY_SPLITS + QUERY_SPLITS)

LANES = 128
HALF = 64
N_PAIRS = 4
NEG = -1e30
LOG2E = 1.4426950408889634
NA_ROW_UNROLL = 4
QUAD = 4
QUAD_W = QUAD * HALF
VMEM_LIMIT = 52 * 1024 * 1024


def _params(n_axes):
    return pltpu.CompilerParams(dimension_semantics=("arbitrary",) * n_axes, vmem_limit_bytes=VMEM_LIMIT)


def _resident(shape):
    nd = len(shape)
    return pl.BlockSpec(shape, lambda *_: (0,) * nd, pipeline_mode=pl.Buffered(1))


def _dot(a, b):
    return jnp.dot(a, b, preferred_element_type=F32)


def _dot_nt(a, b):
    return lax.dot_general(a, b, (((1,), (1,)), ((), ())), preferred_element_type=F32)


def _silu(x):
    return x * jax.nn.sigmoid(x)


def _rms(x, g):
    return x * lax.rsqrt(jnp.mean(x * x, axis=-1, keepdims=True) + EPS) * g


def _lane(shape=(1, LANES)):
    return lax.broadcasted_iota(jnp.int32, shape, len(shape) - 1)


def _ada_kernel(c_ref, w_ref, b_ref, o_ref):
    a = _silu(c_ref[...]).astype(BF16)
    o_ref[...] = _dot(a, w_ref[...].astype(BF16)) + b_ref[...]


def _ada(c_rows, w_ada, b_ada):
    r, d = c_rows.shape
    n = w_ada.shape[1]
    tn = 1536
    return pl.pallas_call(
        _ada_kernel,
        grid=(n // tn,),
        in_specs=[pl.BlockSpec((r, d), lambda j: (0, 0)),
                  pl.BlockSpec((d, tn), lambda j: (0, j)),
                  pl.BlockSpec((1, tn), lambda j: (0, j))],
        out_specs=pl.BlockSpec((r, tn), lambda j: (0, j)),
        out_shape=jax.ShapeDtypeStruct((r, n), F32),
        compiler_params=_params(1),
        name="ada",
    )(c_rows, w_ada, b_ada.reshape(1, n))


def _in_proj_kernel(*refs, n_groups):
    x_ref, g_ref, sc_ref, sh_ref = refs[:4]
    w_refs = refs[4:4 + n_groups]
    o_refs = refs[4 + n_groups:]
    h = _rms(x_ref[...], g_ref[...]) * (1.0 + sc_ref[0]) + sh_ref[0]
    hb = h.astype(BF16)
    for w_ref, o_ref in zip(w_refs, o_refs):
        width = w_ref.shape[1]
        for c0 in range(0, width, 512):
            cw = min(512, width - c0)
            o_ref[:, c0:c0 + cw] = _dot(hb, w_ref[:, c0:c0 + cw]).astype(o_ref.dtype)


def _in_proj(x2, g, sc, sh, tiles_per_mod, weights, out_dtypes, tm=512):
    m, d = x2.shape
    mod_map = (lambda i: (0, 0, 0)) if tiles_per_mod is None else (lambda i: (i // tiles_per_mod, 0, 0))
    in_specs = [pl.BlockSpec((tm, d), lambda i: (i, 0)), _resident((1, d)),
                pl.BlockSpec((1, 1, d), mod_map), pl.BlockSpec((1, 1, d), mod_map)]
    in_specs += [_resident(w.shape) for w in weights]
    out_specs = [pl.BlockSpec((tm, w.shape[1]), lambda i: (i, 0)) for w in weights]
    out_shape = [jax.ShapeDtypeStruct((m, w.shape[1]), dt) for w, dt in zip(weights, out_dtypes)]
    return pl.pallas_call(
        functools.partial(_in_proj_kernel, n_groups=len(weights)),
        grid=(m // tm,),
        in_specs=in_specs, out_specs=out_specs, out_shape=out_shape,
        compiler_params=_params(1),
        name="in_proj",
    )(x2, g.reshape(1, d), sc, sh, *weights)


def _na_kernel(q_ref, k_ref, v_ref, kc_ref, vc_ref, bias_ref, o_ref):
    lane = _lane()
    lo = lane < HALF
    kc = kc_ref[...]
    vc = vc_ref[...]
    qscale = NA_HEAD_DIM ** -0.5 * LOG2E
    win = NA_ROW_WIN * GRID_W

    def one_row(r):
        rs = jnp.clip(r - NA_ROW_WIN // 2, 0, GRID_ROWS - NA_ROW_WIN)
        q = q_ref[pl.ds(pl.multiple_of(r * GRID_W, GRID_W), GRID_W), :]
        q = (q.astype(F32) * qscale).astype(q.dtype)
        kw = k_ref[pl.ds(pl.multiple_of(rs * GRID_W, GRID_W), win), :]
        vw = v_ref[pl.ds(pl.multiple_of(rs * GRID_W, GRID_W), win), :]
        zero = jnp.zeros_like(q)
        q2 = jnp.concatenate([jnp.where(lo, q, zero), jnp.where(lo, zero, q)], axis=0)
        s_loc = _dot_nt(q2, kw) + bias_ref[0, r - rs]
        s_ctx = _dot_nt(q2, kc)
        m = jnp.maximum(jnp.max(s_loc, axis=-1, keepdims=True), jnp.max(s_ctx, axis=-1, keepdims=True))
        p_loc = jnp.exp2(s_loc - m)
        p_ctx = jnp.exp2(s_ctx - m)
        den = jnp.sum(p_loc, axis=-1, keepdims=True) + jnp.sum(p_ctx, axis=-1, keepdims=True)
        o = (_dot(p_loc.astype(BF16), vw) + _dot(p_ctx.astype(BF16), vc)) / den
        o_ref[pl.ds(pl.multiple_of(r * GRID_W, GRID_W), GRID_W), :] = jnp.where(lo, o[:GRID_W], o[GRID_W:]).astype(o_ref.dtype)

    def rows(i, carry):
        for u in range(NA_ROW_UNROLL):
            one_row(i * NA_ROW_UNROLL + u)
        return carry

    lax.fori_loop(0, GRID_ROWS // NA_ROW_UNROLL, rows, 0)


def _na_bias_table(rpb):
    w = np.arange(GRID_W)
    j = np.arange(GRID_W)
    cs = np.clip(w - NA_COL_WIN // 2, 0, GRID_W - NA_COL_WIN)
    valid = (j[None, :] >= cs[:, None]) & (j[None, :] < cs[:, None] + NA_COL_WIN)
    col = np.clip(j[None, :] - w[:, None] + NA_COL_WIN - 1, 0, 2 * NA_COL_WIN - 2)
    n_rho = 2 * NA_ROW_WIN - 1
    t = jnp.where(valid[None, None], rpb.astype(F32)[:, :, col] * LOG2E, NEG)
    tt = t.transpose(0, 2, 1, 3).reshape(NA_HEADS, GRID_W, n_rho * GRID_W)
    win = NA_ROW_WIN * GRID_W
    per_d = [tt[:, :, (NA_ROW_WIN - 1 - d) * GRID_W:(NA_ROW_WIN - 1 - d) * GRID_W + win] for d in range(NA_ROW_WIN)]
    b = jnp.stack(per_d, axis=1)
    b = b.reshape(N_PAIRS, 2, NA_ROW_WIN, GRID_W, win).transpose(0, 2, 1, 3, 4)
    return b.reshape(N_PAIRS, NA_ROW_WIN, 2 * GRID_W, win)


def _na_attention(o_na_l, o_na_c, bias, nb):
    return pl.pallas_call(
        _na_kernel,
        grid=(nb, N_PAIRS),
        in_specs=[pl.BlockSpec((SEQ, LANES), lambda b, p: (b, p)),
                  pl.BlockSpec((SEQ, LANES), lambda b, p: (b, N_PAIRS + p)),
                  pl.BlockSpec((SEQ, LANES), lambda b, p: (b, 2 * N_PAIRS + p)),
                  pl.BlockSpec((CTX_LEN, LANES), lambda b, p: (b, N_PAIRS + p)),
                  pl.BlockSpec((CTX_LEN, LANES), lambda b, p: (b, 2 * N_PAIRS + p)),
                  pl.BlockSpec((1, NA_ROW_WIN, 2 * GRID_W, NA_ROW_WIN * GRID_W), lambda b, p: (p, 0, 0, 0))],
        out_specs=pl.BlockSpec((SEQ, LANES), lambda b, p: (b, p)),
        out_shape=jax.ShapeDtypeStruct((nb * SEQ, NA_HEADS * NA_HEAD_DIM), BF16),
        compiler_params=_params(2),
        name="na_attn",
    )(o_na_l, o_na_l, o_na_l, o_na_c, o_na_c, bias)


def _attn_kernel(*refs, n_src, dq, scale):
    q_ref = refs[0]
    o_ref = refs[-1]
    lo = _lane() < HALF
    lane_o = _lane((1, QUAD_W))
    q = q_ref[...]
    if scale != 1.0:
        q = (q.astype(F32) * scale).astype(q.dtype)
    outs = []
    for a in range(QUAD):
        c0 = a * LANES if dq == LANES else (a // 2) * LANES
        qa = q[:, c0:c0 + LANES]
        if dq != LANES:
            zero = jnp.zeros_like(qa)
            qa = jnp.where(lo, qa, zero) if a % 2 == 0 else jnp.where(lo, zero, qa)
        ss = [_dot_nt(qa, refs[1 + 2 * s][:, c0:c0 + LANES]) for s in range(n_src)]
        m = ss[0].max(axis=-1, keepdims=True)
        for sc in ss[1:]:
            m = jnp.maximum(m, sc.max(axis=-1, keepdims=True))
        den = None
        o = None
        for s in range(n_src):
            p = jnp.exp2(ss[s] - m)
            ps = jnp.sum(p, axis=-1, keepdims=True)
            po = _dot(p.astype(BF16), refs[2 + 2 * s][...])
            den = ps if den is None else den + ps
            o = po if o is None else o + po
        outs.append(o / den)
    out = outs[QUAD - 1]
    for a in range(QUAD - 2, -1, -1):
        out = jnp.where(lane_o < (a + 1) * HALF, outs[a], out)
    o_ref[...] = out.astype(o_ref.dtype)


def _attention(q, srcs, *, nb, t, dq, tq, scale=1.0):
    nq = t // tq
    n_quads = 2 * N_PAIRS // QUAD
    in_specs = [pl.BlockSpec((tq, QUAD * dq), lambda b, p, i: (b * nq + i, p))]
    args = [q]
    for k_arr, v_arr, u, k_off, v_off in srcs:
        in_specs.append(pl.BlockSpec((u, QUAD * dq), lambda b, p, i, k_off=k_off: (b, k_off + p)))
        in_specs.append(pl.BlockSpec((u, QUAD_W), lambda b, p, i, v_off=v_off: (b, v_off + p)))
        args += [k_arr, v_arr]
    return pl.pallas_call(
        functools.partial(_attn_kernel, n_src=len(srcs), dq=dq, scale=scale),
        grid=(nb, n_quads, nq),
        in_specs=in_specs,
        out_specs=pl.BlockSpec((tq, QUAD_W), lambda b, p, i: (b * nq + i, p)),
        out_shape=jax.ShapeDtypeStruct((nb * t, n_quads * QUAD_W), BF16),
        compiler_params=_params(3),
        name="attn",
    )(*args)


def _mla_prep_kernel(*refs, rotate, scale):
    if rotate:
        x_ref, gq_ref, gkv_ref, wq_ref, wk_ref, wv_ref, cos_ref, sin_ref, q_ref, k_ref, v_ref = refs
    else:
        x_ref, gq_ref, gkv_ref, wq_ref, wk_ref, wv_ref, q_ref, k_ref, v_ref = refs
    x = x_ref[...].astype(F32)
    cq = x[:, :MLA_Q_RANK]
    ckv = x[:, MLA_Q_RANK:MLA_Q_RANK + MLA_KV_RANK]
    kr = x[:, MLA_Q_RANK + MLA_KV_RANK:]
    cqn = _rms(cq, gq_ref[...]).astype(BF16)
    ckvn = _rms(ckv, gkv_ref[...]).astype(BF16)
    lane = _lane()
    first_half = lane < MLA_NOPE + MLA_ROPE // 2

    def rope(t):
        if not rotate:
            return t
        rot = jnp.where(first_half, pltpu.roll(t, LANES - MLA_ROPE // 2, axis=1), pltpu.roll(t, MLA_ROPE // 2, axis=1))
        return t * cos_ref[...] + rot * sin_ref[...]

    kr = rope(kr)
    for h in range(MLA_HEADS):
        sl = slice(h * LANES, (h + 1) * LANES)
        qh = rope(_dot(cqn, wq_ref[:, sl]))
        q_ref[:, sl] = (qh * scale).astype(q_ref.dtype)
        k_ref[:, sl] = (_dot(ckvn, wk_ref[:, sl]) + kr).astype(k_ref.dtype)
    v_ref[...] = _dot(ckvn, wv_ref[...]).astype(v_ref.dtype)


def _mla_prep(o_mla, gq, gkv, wq, wk, wv, tables, tm=512):
    m = o_mla.shape[0]
    rotate = tables is not None
    scale = (MLA_NOPE + MLA_ROPE) ** -0.5 * LOG2E
    in_specs = [pl.BlockSpec((tm, o_mla.shape[1]), lambda i: (i, 0)),
                _resident((1, MLA_Q_RANK)), _resident((1, MLA_KV_RANK)),
                _resident(wq.shape), _resident(wk.shape), _resident(wv.shape)]
    args = [o_mla, gq.reshape(1, -1), gkv.reshape(1, -1), wq, wk, wv]
    if rotate:
        nt = SEQ // tm
        in_specs += [pl.BlockSpec((tm, LANES), lambda i: (i % nt, 0))] * 2
        args += list(tables)
    widths = (MLA_HEADS * LANES, MLA_HEADS * LANES, MLA_HEADS * MLA_V)
    return pl.pallas_call(
        functools.partial(_mla_prep_kernel, rotate=rotate, scale=scale),
        grid=(m // tm,),
        in_specs=in_specs,
        out_specs=[pl.BlockSpec((tm, w), lambda i: (i, 0)) for w in widths],
        out_shape=[jax.ShapeDtypeStruct((m, w), BF16) for w in widths],
        compiler_params=_params(1),
        name="mla_prep",
    )(*args)


def _gqa_prep_kernel(*refs, rotate, scale):
    if rotate:
        x_ref, gq_ref, gk_ref, bd_ref, cos_ref, sin_ref, q_ref, k_ref, v_ref = refs
    else:
        x_ref, gq_ref, gk_ref, bd_ref, q_ref, k_ref, v_ref = refs
    x = x_ref[...].astype(F32)
    lane = _lane()
    lo = lane < HALF
    first_half = (lane & (GQA_HEAD_DIM - 1)) < GQA_HEAD_DIM // 2

    def headnorm(t, g):
        ss = _dot((t * t).astype(BF16), bd_ref[...]) * (1.0 / GQA_HEAD_DIM)
        return t * lax.rsqrt(ss + EPS) * g

    def rope(t):
        if not rotate:
            return t
        rot = jnp.where(first_half, pltpu.roll(t, LANES - GQA_HEAD_DIM // 2, axis=1), pltpu.roll(t, GQA_HEAD_DIM // 2, axis=1))
        return t * cos_ref[...] + rot * sin_ref[...]

    for j in range(N_PAIRS):
        sl = slice(j * LANES, (j + 1) * LANES)
        q_ref[:, sl] = (rope(headnorm(x[:, sl], gq_ref[...])) * scale).astype(q_ref.dtype)
    nq = GQA_Q_HEADS * GQA_HEAD_DIM
    k = rope(headnorm(x[:, nq:nq + LANES], gk_ref[...]))
    v = x[:, nq + LANES:nq + 2 * LANES]
    for t, ref in ((k, k_ref), (v, v_ref)):
        sw = pltpu.roll(t, HALF, axis=1)
        for g, dup in enumerate((jnp.where(lo, t, sw), jnp.where(lo, sw, t))):
            dup = dup.astype(ref.dtype)
            ref[:, (2 * g) * LANES:(2 * g + 1) * LANES] = dup
            ref[:, (2 * g + 1) * LANES:(2 * g + 2) * LANES] = dup


def _gqa_prep(o_gqa, gq, gk, tables, tm=512):
    m = o_gqa.shape[0]
    rotate = tables is not None
    bd = jnp.asarray(np.kron(np.eye(2), np.ones((HALF, HALF))), BF16)
    in_specs = [pl.BlockSpec((tm, o_gqa.shape[1]), lambda i: (i, 0)),
                _resident((1, LANES)), _resident((1, LANES)), _resident((LANES, LANES))]
    args = [o_gqa, jnp.tile(gq, 2).reshape(1, LANES), jnp.tile(gk, 2).reshape(1, LANES), bd]
    if rotate:
        nt = SEQ // tm
        in_specs += [pl.BlockSpec((tm, LANES), lambda i: (i % nt, 0))] * 2
        args += list(tables)
    widths = (GQA_Q_HEADS * GQA_HEAD_DIM, GQA_KV_HEADS * QUAD_W, GQA_KV_HEADS * QUAD_W)
    return pl.pallas_call(
        functools.partial(_gqa_prep_kernel, rotate=rotate, scale=GQA_HEAD_DIM ** -0.5 * LOG2E),
        grid=(m // tm,),
        in_specs=in_specs,
        out_specs=[pl.BlockSpec((tm, w), lambda i: (i, 0)) for w in widths],
        out_shape=[jax.ShapeDtypeStruct((m, w), BF16) for w in widths],
        compiler_params=_params(1),
        name="gqa_prep",
    )(*args)


def _rope_tables(dim, lane_start, group):
    t = jnp.arange(SEQ)
    row = (t // GRID_W).astype(F32)
    col = (t % GRID_W).astype(F32)
    quarter = dim // 4
    inv = ROPE_THETA ** (-jnp.arange(quarter, dtype=F32) / quarter)
    ang = jnp.concatenate([row[:, None] * inv, col[:, None] * inv], axis=-1)
    cos, sin = jnp.cos(ang), jnp.sin(ang)
    cos_g = jnp.ones((SEQ, group), F32).at[:, lane_start:lane_start + dim].set(jnp.concatenate([cos, cos], axis=-1))
    sin_g = jnp.zeros((SEQ, group), F32).at[:, lane_start:lane_start + dim].set(jnp.concatenate([-sin, sin], axis=-1))
    reps = LANES // group
    return jnp.tile(cos_g, (1, reps)), jnp.tile(sin_g, (1, reps))


CONV_HALO = 16


def _conv_kernel(x_ref, prev_ref, next_ref, w_ref, b_ref, o_ref):
    i = pl.program_id(1)
    n = pl.num_programs(1)
    tm = x_ref.shape[0]
    rows = tm + 2 * CONV_HALO
    for c0 in range(0, x_ref.shape[1], LANES):
        sl = slice(c0, c0 + LANES)
        prev = jnp.where(i > 0, prev_ref[:, sl].astype(F32), 0.0)
        nxt = jnp.where(i < n - 1, next_ref[:, sl].astype(F32), 0.0)
        xcat = jnp.concatenate([prev, x_ref[:, sl].astype(F32), nxt], axis=0)
        acc = jnp.zeros((tm, LANES), F32) + b_ref[:, sl]
        for k in range(SSM_CONV):
            sh = (SSM_CONV // 2 - k) % rows
            rolled = xcat if sh == 0 else pltpu.roll(xcat, sh, axis=0)
            acc = acc + rolled[CONV_HALO:CONV_HALO + tm] * w_ref[k:k + 1, sl]
        o_ref[:, sl] = _silu(acc).astype(o_ref.dtype)


def _ssm_conv(o_ssm, conv_w, conv_b, nb, t, tm=256):
    nch = conv_w.shape[1]
    nt = t // tm
    hb = tm // CONV_HALO
    total = nb * t // CONV_HALO
    w8 = jnp.zeros((8, nch), F32).at[:SSM_CONV].set(conv_w)
    return pl.pallas_call(
        _conv_kernel,
        grid=(nb, nt),
        in_specs=[pl.BlockSpec((tm, nch), lambda b, i: (b * nt + i, 0)),
                  pl.BlockSpec((CONV_HALO, nch), lambda b, i: (jnp.maximum((b * nt + i) * hb - 1, 0), 0)),
                  pl.BlockSpec((CONV_HALO, nch), lambda b, i: (jnp.minimum((b * nt + i + 1) * hb, total - 1), 0)),
                  _resident((8, nch)), _resident((1, nch))],
        out_specs=pl.BlockSpec((tm, nch), lambda b, i: (b * nt + i, 0)),
        out_shape=jax.ShapeDtypeStruct((nb * t, nch), BF16),
        compiler_params=_params(2),
        name="ssm_conv",
    )(o_ssm, o_ssm, o_ssm, w8, conv_b.reshape(1, nch))


def _softplus(x):
    return jnp.maximum(x, 0.0) + jnp.log(1.0 + jnp.exp(-jnp.abs(x)))


def _ssd_kernel(*refs, with_y):
    n_dir = 6 if with_y else 5
    dir_refs = (refs[:n_dir], refs[n_dir:2 * n_dir])
    al_ref, alt_ref, bi_ref, bit_ref, h0_ref = refs[2 * n_dir:2 * n_dir + 5]
    if with_y:
        yf_ref, yb_ref, hf_ref, st = refs[2 * n_dir + 5:]
        y_refs = (yf_ref, yb_ref)
    else:
        hf_ref, st = refs[2 * n_dir + 5:]
    j = pl.program_id(1)
    q = SSM_CHUNK

    @pl.when(j == 0)
    def _():
        st[...] = h0_ref[:, 0]

    row = lax.broadcasted_iota(jnp.int32, (q, q), 0)
    col = lax.broadcasted_iota(jnp.int32, (q, q), 1)
    lane = _lane()
    lo = lane < HALF

    def col_of(x, h):
        return jnp.sum(jnp.where(lane == h, x, 0.0), axis=1, keepdims=True)

    for d in range(2):
        if with_y:
            x_ref, b_ref, c_ref, bt_ref, dt_ref, dtt_ref = dir_refs[d]
        else:
            x_ref, b_ref, bt_ref, dt_ref, dtt_ref = dir_refs[d]
        m_qs = (col <= row) if d == 0 else (col >= row)
        m_ts = (row <= col) if d == 0 else (row >= col)
        dt = _softplus(dt_ref[0] + bi_ref[d])
        da = jnp.where(lane < SSM_HEADS, dt * (-jnp.exp(al_ref[d])), 0.0)
        a_cum_c = jnp.dot(jnp.where(m_qs, 1.0, 0.0), da, precision=HIGHEST, preferred_element_type=F32)
        a_tot = jnp.sum(da, axis=0, keepdims=True)
        dat = _softplus(dtt_ref[0, 0] + bit_ref[d]) * (-jnp.exp(alt_ref[d]))
        a_cum_r = jnp.dot(dat, jnp.where(m_ts, 1.0, 0.0), precision=HIGHEST, preferred_element_type=F32)

        x = x_ref[...].astype(F32)
        bt = bt_ref[0]
        if with_y:
            cm = c_ref[...]
        for g in range(SSM_GROUPS):
            btg = bt[g * SSM_STATE:(g + 1) * SSM_STATE, :]
            if with_y:
                cg = cm[:, g * SSM_STATE:(g + 1) * SSM_STATE]
                cb = _dot(cg, btg)
            for pp in range(2):
                p = 2 * g + pp
                h0, h1 = 2 * p, 2 * p + 1
                ac0, ac1 = col_of(a_cum_c, h0), col_of(a_cum_c, h1)
                tot0, tot1 = col_of(a_tot, h0), col_of(a_tot, h1)
                xdt = x[:, p * LANES:(p + 1) * LANES] * jnp.where(lo, col_of(dt, h0), col_of(dt, h1))
                ht = st[d, p]
                if with_y:
                    xdt_b = xdt.astype(BF16)
                    l0 = jnp.exp(jnp.where(m_qs, ac0 - a_cum_r[h0:h0 + 1, :], NEG))
                    l1 = jnp.exp(jnp.where(m_qs, ac1 - a_cum_r[h1:h1 + 1, :], NEG))
                    y_in = jnp.where(lo, _dot((cb * l0).astype(BF16), xdt_b), _dot((cb * l1).astype(BF16), xdt_b))
                    y_st = _dot(cg, ht.astype(BF16)) * jnp.where(lo, jnp.exp(ac0), jnp.exp(ac1))
                    y_refs[d][:, p * LANES:(p + 1) * LANES] = (y_in + y_st).astype(y_refs[d].dtype)
                dec = jnp.where(lo, jnp.exp(tot0 - ac0), jnp.exp(tot1 - ac1))
                st[d, p] = ht * jnp.where(lo, jnp.exp(tot0), jnp.exp(tot1)) + _dot(btg, (xdt * dec).astype(BF16))

    @pl.when(j == pl.num_programs(1) - 1)
    def _():
        hf_ref[:, 0] = st[...]


def _ssd(xbc, b_t, dt2, dtt2, a_log, dt_bias, h0, *, nb, t, with_y):
    nc = t // SSM_CHUNK
    pad = LANES - SSM_HEADS
    al_row = jnp.pad(a_log.astype(F32), ((0, 0), (0, pad))).reshape(2, 1, LANES)
    bi_row = jnp.pad(dt_bias.astype(F32), ((0, 0), (0, pad))).reshape(2, 1, LANES)
    al_t = jnp.broadcast_to(a_log.astype(F32)[:, :, None], (2, SSM_HEADS, SSM_CHUNK))
    bi_t = jnp.broadcast_to(dt_bias.astype(F32)[:, :, None], (2, SSM_HEADS, SSM_CHUNK))
    x_blk = SSM_INNER // SSM_BC
    in_specs, args = [], []
    for d in range(2):
        cidx = (lambda j: j) if d == 0 else (lambda j: nc - 1 - j)
        in_specs += [pl.BlockSpec((SSM_CHUNK, SSM_INNER), lambda b, j, c=cidx: (b * nc + c(j), 0)),
                     pl.BlockSpec((SSM_CHUNK, SSM_BC), lambda b, j, c=cidx: (b * nc + c(j), x_blk))]
        args += [xbc, xbc]
        if with_y:
            in_specs.append(pl.BlockSpec((SSM_CHUNK, SSM_BC), lambda b, j, c=cidx: (b * nc + c(j), x_blk + 1)))
            args.append(xbc)
        in_specs += [pl.BlockSpec((1, SSM_BC, SSM_CHUNK), lambda b, j, c=cidx: (b, 0, c(j))),
                     pl.BlockSpec((1, SSM_CHUNK, LANES), lambda b, j, c=cidx, d=d: (d, b * nc + c(j), 0)),
                     pl.BlockSpec((1, 1, SSM_HEADS, SSM_CHUNK), lambda b, j, c=cidx, d=d: (d, b, 0, c(j)))]
        args += [b_t, dt2, dtt2]
    st_spec = pl.BlockSpec((2, 1, N_PAIRS, SSM_STATE, LANES), lambda b, j: (0, b, 0, 0, 0))
    in_specs += [_resident(al_row.shape), _resident(al_t.shape), _resident(bi_row.shape), _resident(bi_t.shape), st_spec]
    args += [al_row, al_t, bi_row, bi_t, h0]
    st_shape = jax.ShapeDtypeStruct((2, nb, N_PAIRS, SSM_STATE, LANES), F32)
    out_specs, out_shape = [st_spec], [st_shape]
    if with_y:
        out_specs = [pl.BlockSpec((SSM_CHUNK, SSM_INNER), lambda b, j: (b * nc + j, 0)),
                     pl.BlockSpec((SSM_CHUNK, SSM_INNER), lambda b, j: (b * nc + nc - 1 - j, 0))] + out_specs
        out_shape = [jax.ShapeDtypeStruct((nb * t, SSM_INNER), BF16)] * 2 + out_shape
    return pl.pallas_call(
        functools.partial(_ssd_kernel, with_y=with_y),
        grid=(nb, nc),
        in_specs=in_specs, out_specs=out_specs, out_shape=out_shape,
        scratch_shapes=[pltpu.VMEM((2, N_PAIRS, SSM_STATE, LANES), F32)],
        compiler_params=_params(2),
        name="ssd",
    )(*args)


def _ssm_combine_kernel(yf_ref, yb_ref, x_ref, z_ref, d_ref, g_ref, o_ref):
    y = yf_ref[...].astype(F32) + yb_ref[...].astype(F32) + d_ref[...] * x_ref[...].astype(F32)
    y = y * _silu(z_ref[...].astype(F32))
    o_ref[...] = _rms(y, g_ref[...]).astype(o_ref.dtype)


def _ssm_combine(y_f, y_b, xbc, o_ssm, d_skip, g_norm, tm=512):
    m = xbc.shape[0]
    dvec = jnp.repeat(d_skip.astype(F32), SSM_HEAD_DIM).reshape(1, SSM_INNER)
    z_blk = o_ssm.shape[1] // SSM_INNER - 1
    return pl.pallas_call(
        _ssm_combine_kernel,
        grid=(m // tm,),
        in_specs=[pl.BlockSpec((tm, SSM_INNER), lambda i: (i, 0)),
                  pl.BlockSpec((tm, SSM_INNER), lambda i: (i, 0)),
                  pl.BlockSpec((tm, SSM_INNER), lambda i: (i, 0)),
                  pl.BlockSpec((tm, SSM_INNER), lambda i: (i, z_blk)),
                  _resident((1, SSM_INNER)), _resident((1, SSM_INNER))],
        out_specs=pl.BlockSpec((tm, SSM_INNER), lambda i: (i, 0)),
        out_shape=jax.ShapeDtypeStruct((m, SSM_INNER), BF16),
        compiler_params=_params(1),
        name="ssm_combine",
    )(y_f, y_b, xbc, o_ssm, dvec, g_norm.reshape(1, SSM_INNER))


def _merge_kernel(yna_ref, ymla_ref, ygqa_ref, yssm_ref, gate_ref, wb_ref, wo_ref, x_ref, gt_ref, g_ref, o_ref):
    acc = None
    for k, y_ref in enumerate((yna_ref, ymla_ref, ygqa_ref, yssm_ref)):
        proj = _dot(y_ref[...], wb_ref[k])
        gate = jax.nn.sigmoid(gate_ref[:, k * D_MODEL:(k + 1) * D_MODEL].astype(F32))
        acc = gate * proj if acc is None else acc + gate * proj
    mix = _dot(acc.astype(BF16), wo_ref[...])
    o_ref[...] = x_ref[...] + gt_ref[0] * _rms(mix, g_ref[...])


def _merge(ys, gate, wb, wo, x2, gt, g_post, tiles_per_mod, tm=512):
    m, d = x2.shape
    mod_map = (lambda i: (0, 0, 0)) if tiles_per_mod is None else (lambda i: (i // tiles_per_mod, 0, 0))
    in_specs = [pl.BlockSpec((tm, BRANCH_W), lambda i: (i, 0))] * N_BRANCH
    in_specs += [pl.BlockSpec((tm, N_BRANCH * d), lambda i: (i, 0)),
                 _resident(wb.shape), _resident(wo.shape),
                 pl.BlockSpec((tm, d), lambda i: (i, 0)),
                 pl.BlockSpec((1, 1, d), mod_map), _resident((1, d))]
    return pl.pallas_call(
        _merge_kernel,
        grid=(m // tm,),
        in_specs=in_specs,
        out_specs=pl.BlockSpec((tm, d), lambda i: (i, 0)),
        out_shape=jax.ShapeDtypeStruct((m, d), F32),
        compiler_params=_params(1),
        name="merge",
    )(*ys, gate, wb, wo, x2, gt, g_post.reshape(1, d))


def _ffn_kernel(x_ref, gpre_ref, sc_ref, sh_ref, w1_ref, w3_ref, w2_ref, gt_ref, gpost_ref, o_ref):
    x = x_ref[...]
    hb = (_rms(x, gpre_ref[...]) * (1.0 + sc_ref[0]) + sh_ref[0]).astype(BF16)
    u = _silu(_dot(hb, w1_ref[...])) * _dot(hb, w3_ref[...])
    f = _dot(u.astype(BF16), w2_ref[...])
    o_ref[...] = x + gt_ref[0] * _rms(f, gpost_ref[...])


def _ffn(x2, g_pre, sc, sh, w1, w3, w2, gt, g_post, tiles_per_mod, tm=512):
    m, d = x2.shape
    mod_map = (lambda i: (0, 0, 0)) if tiles_per_mod is None else (lambda i: (i // tiles_per_mod, 0, 0))
    mod = pl.BlockSpec((1, 1, d), mod_map)
    return pl.pallas_call(
        _ffn_kernel,
        grid=(m // tm,),
        in_specs=[pl.BlockSpec((tm, d), lambda i: (i, 0)), _resident((1, d)), mod, mod,
                  _resident(w1.shape), _resident(w3.shape), _resident(w2.shape), mod, _resident((1, d))],
        out_specs=pl.BlockSpec((tm, d), lambda i: (i, 0)),
        out_shape=jax.ShapeDtypeStruct((m, d), F32),
        compiler_params=_params(1),
        name="ffn",
    )(x2, g_pre.reshape(1, d), sc, sh, w1, w3, w2, gt, g_post.reshape(1, d))


def _split_w_in(w_in):
    off, cols = 0, {}
    for name, w in KEY_SPLITS + QUERY_SPLITS:
        cols[name] = w_in[:, off:off + w]
        off += w
    d = w_in.shape[0]
    z = lambda n: jnp.zeros((d, n), w_in.dtype)
    cat = lambda parts: jnp.concatenate(parts, axis=1).astype(BF16)
    w_na = cat([cols['na_q'], cols['na_k'], cols['na_v']])
    w_mla = cat([cols['mla_cq'], cols['mla_ckv'], z(MLA_NOPE), cols['mla_kr'], z(LANES - MLA_NOPE - MLA_ROPE)])
    w_gqa = cat([cols['gqa_q'], cols['gqa_k'], cols['gqa_v']])
    w_ssm = cat([cols['ssm_x'], cols['ssm_B'], cols['ssm_C'], cols['ssm_z']])
    w_dt = cat([cols['ssm_dt'], z(LANES - 2 * SSM_HEADS)])
    w_gate = w_in[:, MIX_COLS:].astype(BF16)
    return w_na, w_mla, w_gqa, w_ssm, w_dt, w_gate


def _dt_layouts(o_dt, nb, t):
    both = o_dt[:, :2 * SSM_HEADS].reshape(nb * t, 2, SSM_HEADS).transpose(1, 0, 2)
    dt2 = jnp.pad(both, ((0, 0), (0, 0), (0, LANES - SSM_HEADS)))
    dtt2 = both.reshape(2, nb, t, SSM_HEADS).transpose(0, 1, 3, 2)
    return dt2, dtt2


def _layer(x2, xc2, mods, mc, lw, tables, nb, need_ctx):
    d = D_MODEL
    tpm = SEQ // 512
    chunk = lambda a, k: a[:, k * d:(k + 1) * d].reshape(-1, 1, d)
    sh1, sc1, gt1, sh2, sc2, gt2 = (chunk(mods, k) for k in range(6))
    csh1, csc1, cgt1, csh2, csc2, cgt2 = (chunk(mc, k) for k in range(6))
    w_na, w_mla, w_gqa, w_ssm, w_dt, w_gate = _split_w_in(lw['w_in'])
    dts = (BF16, BF16, BF16, BF16, F32, BF16)
    o_na, o_mla, o_gqa, o_ssm, o_dt, gate = _in_proj(
        x2, lw['g_pre1'], sc1, sh1, tpm, (w_na, w_mla, w_gqa, w_ssm, w_dt, w_gate), dts)
    if need_ctx:
        c_na, c_mla, c_gqa, c_ssm, c_dt, c_gate = _in_proj(
            xc2, lw['g_pre1'], csc1, csh1, None, (w_na, w_mla, w_gqa, w_ssm, w_dt, w_gate), dts)
    else:
        c_na, c_mla, c_gqa, c_ssm, c_dt = _in_proj(
            xc2, lw['g_pre1'], csc1, csh1, None, (w_na, w_mla, w_gqa, w_ssm, w_dt), dts[:5])

    y_na = _na_attention(o_na, c_na, _na_bias_table(lw['na_rpb']), nb)

    wq = jnp.pad(lw['mla_w_uq'].reshape(MLA_Q_RANK, MLA_HEADS, MLA_NOPE + MLA_ROPE),
                 ((0, 0), (0, 0), (0, LANES - MLA_NOPE - MLA_ROPE))).reshape(MLA_Q_RANK, MLA_HEADS * LANES).astype(BF16)
    wkv = lw['mla_w_ukv'].reshape(MLA_KV_RANK, MLA_HEADS, MLA_NOPE + MLA_V)
    wk = jnp.pad(wkv[:, :, :MLA_NOPE], ((0, 0), (0, 0), (0, LANES - MLA_NOPE))).reshape(MLA_KV_RANK, MLA_HEADS * LANES).astype(BF16)
    wv = wkv[:, :, MLA_NOPE:].reshape(MLA_KV_RANK, MLA_HEADS * MLA_V).astype(BF16)
    ql, kl, vl = _mla_prep(o_mla, lw['mla_g_q'], lw['mla_g_kv'], wq, wk, wv, tables['mla'])
    qc, kc, vc = _mla_prep(c_mla, lw['mla_g_q'], lw['mla_g_kv'], wq, wk, wv, None)
    y_mla = _attention(ql, [(kc, vc, CTX_LEN, 0, 0), (kl, vl, SEQ, 0, 0)], nb=nb, t=SEQ, dq=LANES, tq=256)

    gql, gkl, gvl = _gqa_prep(o_gqa, lw['gqa_g_q'], lw['gqa_g_k'], tables['gqa'])
    gqc, gkc, gvc = _gqa_prep(c_gqa, lw['gqa_g_q'], lw['gqa_g_k'], None)
    y_gqa = _attention(gql, [(gkc, gvc, CTX_LEN, 0, 0), (gkl, gvl, SEQ, 0, 0)], nb=nb, t=SEQ, dq=HALF, tq=256)

    nxbc = SSM_INNER + 2 * SSM_BC
    conv_w, conv_b = lw['ssm_conv_w'], lw['ssm_conv_b']
    xbc_l = _ssm_conv(o_ssm, conv_w, conv_b, nb, SEQ)
    xbc_c = _ssm_conv(c_ssm, conv_w, conv_b, nb, CTX_LEN)
    bt_of = lambda xbc, t: xbc[:, SSM_INNER:SSM_INNER + SSM_BC].reshape(nb, t, SSM_BC).transpose(0, 2, 1)
    h_zero = jnp.zeros((2, nb, N_PAIRS, SSM_STATE, LANES), F32)
    dt2c, dtt2c = _dt_layouts(c_dt, nb, CTX_LEN)
    res_c = _ssd(xbc_c, bt_of(xbc_c, CTX_LEN), dt2c, dtt2c, lw['ssm_a_log'], lw['ssm_dt_bias'], h_zero,
                 nb=nb, t=CTX_LEN, with_y=need_ctx)
    h_ctx = res_c[-1]
    dt2l, dtt2l = _dt_layouts(o_dt, nb, SEQ)
    yf_l, yb_l, _ = _ssd(xbc_l, bt_of(xbc_l, SEQ), dt2l, dtt2l, lw['ssm_a_log'], lw['ssm_dt_bias'], h_ctx,
                         nb=nb, t=SEQ, with_y=True)
    y_ssm = _ssm_combine(yf_l, yb_l, xbc_l, o_ssm, lw['ssm_d'], lw['ssm_g_norm'])

    wb = lw['w_branch'].astype(BF16)
    wo = lw['w_out'].astype(BF16)
    w1, w3, w2 = lw['ffn_w1'].astype(BF16), lw['ffn_w3'].astype(BF16), lw['ffn_w2'].astype(BF16)
    x2 = _merge((y_na, y_mla, y_gqa, y_ssm), gate, wb, wo, x2, gt1, lw['g_post1'], tpm)
    x2 = _ffn(x2, lw['g_pre2'], sc2, sh2, w1, w3, w2, gt2, lw['g_post2'], tpm)

    if need_ctx:
        yc_na = _attention(c_na, [(c_na, c_na, CTX_LEN, 2, 4)], nb=nb, t=CTX_LEN, dq=HALF,
                           tq=CTX_LEN, scale=NA_HEAD_DIM ** -0.5 * LOG2E)
        yc_mla = _attention(qc, [(kc, vc, CTX_LEN, 0, 0)], nb=nb, t=CTX_LEN, dq=LANES, tq=CTX_LEN)
        yc_gqa = _attention(gqc, [(gkc, gvc, CTX_LEN, 0, 0)], nb=nb, t=CTX_LEN, dq=HALF, tq=CTX_LEN)
        yc_ssm = _ssm_combine(res_c[0], res_c[1], xbc_c, c_ssm, lw['ssm_d'], lw['ssm_g_norm'])
        xc2 = _merge((yc_na, yc_mla, yc_gqa, yc_ssm), c_gate, wb, wo, xc2, cgt1, lw['g_post1'], None)
        xc2 = _ffn(xc2, lw['g_pre2'], csc2, csh2, w1, w3, w2, cgt2, lw['g_post2'], None)
    return x2, xc2


def kernel(x, c, ctx, c_ctx, w_ada, b_ada, g_pre1, g_post1, g_pre2, g_post2, w_in, na_rpb, mla_g_q, mla_g_kv, mla_w_uq, mla_w_ukv, gqa_g_q, gqa_g_k, ssm_conv_w, ssm_conv_b, ssm_a_log, ssm_dt_bias, ssm_d, ssm_g_norm, w_branch, w_out, ffn_w1, ffn_w3, ffn_w2):
    nb, s, d = x.shape
    assert s == SEQ and d == D_MODEL and ctx.shape[1] == CTX_LEN
    depth = w_ada.shape[0]
    tables = {'mla': _rope_tables(MLA_ROPE, MLA_NOPE, LANES), 'gqa': _rope_tables(GQA_HEAD_DIM, 0, GQA_HEAD_DIM)}
    x2 = x.reshape(nb * s, d)
    xc2 = ctx.reshape(nb * CTX_LEN, d)
    n_rows = -(-(nb + 1) // 8) * 8
    c_rows = jnp.zeros((n_rows, d), F32).at[:nb].set(c).at[nb].set(c_ctx)
    stacked = dict(w_in=w_in, na_rpb=na_rpb, mla_g_q=mla_g_q, mla_g_kv=mla_g_kv, mla_w_uq=mla_w_uq,
                   mla_w_ukv=mla_w_ukv, gqa_g_q=gqa_g_q, gqa_g_k=gqa_g_k, ssm_conv_w=ssm_conv_w,
                   ssm_conv_b=ssm_conv_b, ssm_a_log=ssm_a_log, ssm_dt_bias=ssm_dt_bias, ssm_d=ssm_d,
                   ssm_g_norm=ssm_g_norm, w_branch=w_branch, w_out=w_out, ffn_w1=ffn_w1, ffn_w3=ffn_w3,
                   ffn_w2=ffn_w2, g_pre1=g_pre1, g_post1=g_post1, g_pre2=g_pre2, g_post2=g_post2)
    for l in range(depth):
        lw = {k: v[l] for k, v in stacked.items()}
        mod_all = _ada(c_rows, w_ada[l], b_ada[l])
        x2, xc2 = _layer(x2, xc2, mod_all[:nb], mod_all[nb:nb + 1], lw, tables, nb, l < depth - 1)
    return x2.reshape(nb, s, d)
```

```python
import functools

import numpy as np
import jax
import jax.numpy as jnp
from jax import lax
from jax.experimental import pallas as pl
from jax.experimental.pallas import tpu as pltpu

F32 = jnp.float32
BF16 = jnp.bfloat16
HIGHEST = lax.Precision.HIGHEST

D_MODEL = 1024
SEQ = 2048
GRID_W = 64
GRID_ROWS = SEQ // GRID_W
CTX_LEN = 256
N_BRANCH = 4
BRANCH_W = 512
ROPE_THETA = 10000.0
EPS = 1e-6
NA_HEADS = 8
NA_HEAD_DIM = 64
NA_ROW_WIN = 8
NA_COL_WIN = 16
MLA_HEADS = 8
MLA_Q_RANK = 384
MLA_KV_RANK = 256
MLA_NOPE = 64
MLA_ROPE = 32
MLA_V = 64
GQA_Q_HEADS = 8
GQA_KV_HEADS = 2
GQA_HEAD_DIM = 64
SSM_HEADS = 8
SSM_HEAD_DIM = 64
SSM_GROUPS = 2
SSM_STATE = 128
SSM_CONV = 5
SSM_CHUNK = 128
SSM_INNER = SSM_HEADS * SSM_HEAD_DIM
SSM_BC = SSM_GROUPS * SSM_STATE
FFN_HIDDEN = ((8 * D_MODEL + 3 * 256 - 1) // (3 * 256)) * 256

KEY_SPLITS = (('na_k', 512), ('na_v', 512), ('mla_ckv', MLA_KV_RANK), ('mla_kr', MLA_ROPE),
              ('gqa_k', 128), ('gqa_v', 128), ('ssm_x', SSM_INNER), ('ssm_B', SSM_BC), ('ssm_dt', 2 * SSM_HEADS))
QUERY_SPLITS = (('na_q', 512), ('mla_cq', MLA_Q_RANK), ('gqa_q', 512), ('ssm_C', SSM_BC), ('ssm_z', SSM_INNER))
MIX_COLS = sum(w for _, w in KEY_SPLITS + QUERY_SPLITS)

LANES = 128
HALF = 64
N_PAIRS = 4
NEG = -1e30
LOG2E = 1.4426950408889634
NA_BLK = 4
NA_BLOCKS = GRID_ROWS // NA_BLK
NA_WIN_ROWS = 12
QUAD = 4
QUAD_W = QUAD * HALF
VMEM_LIMIT = 52 * 1024 * 1024


def _params(n_axes):
    return pltpu.CompilerParams(dimension_semantics=("arbitrary",) * n_axes, vmem_limit_bytes=VMEM_LIMIT)


def _resident(shape):
    nd = len(shape)
    return pl.BlockSpec(shape, lambda *_: (0,) * nd, pipeline_mode=pl.Buffered(1))


def _dot(a, b):
    return jnp.dot(a, b, preferred_element_type=F32)


def _dot_nt(a, b):
    return lax.dot_general(a, b, (((1,), (1,)), ((), ())), preferred_element_type=F32)


def _silu(x):
    return x * jax.nn.sigmoid(x)


def _rms(x, g):
    return x * lax.rsqrt(jnp.mean(x * x, axis=-1, keepdims=True) + EPS) * g


def _lane(shape=(1, LANES)):
    return lax.broadcasted_iota(jnp.int32, shape, len(shape) - 1)


def _ada_kernel(c_ref, w_ref, b_ref, o_ref):
    a = _silu(c_ref[...]).astype(BF16)
    o_ref[...] = _dot(a, w_ref[...].astype(BF16)) + b_ref[...]


def _ada(c_rows, w_ada, b_ada):
    r, d = c_rows.shape
    n = w_ada.shape[1]
    tn = 1536
    return pl.pallas_call(
        _ada_kernel,
        grid=(n // tn,),
        in_specs=[pl.BlockSpec((r, d), lambda j: (0, 0)),
                  pl.BlockSpec((d, tn), lambda j: (0, j)),
                  pl.BlockSpec((1, tn), lambda j: (0, j))],
        out_specs=pl.BlockSpec((r, tn), lambda j: (0, j)),
        out_shape=jax.ShapeDtypeStruct((r, n), F32),
        compiler_params=_params(1),
        name="ada",
    )(c_rows, w_ada, b_ada.reshape(1, n))


def _in_proj_kernel(*refs, n_groups):
    x_ref, g_ref, sc_ref, sh_ref = refs[:4]
    w_refs = refs[4:4 + n_groups]
    o_refs = refs[4 + n_groups:]
    h = _rms(x_ref[...], g_ref[...]) * (1.0 + sc_ref[0]) + sh_ref[0]
    hb = h.astype(BF16)
    for w_ref, o_ref in zip(w_refs, o_refs):
        width = w_ref.shape[1]
        for c0 in range(0, width, 512):
            cw = min(512, width - c0)
            o_ref[:, c0:c0 + cw] = _dot(hb, w_ref[:, c0:c0 + cw]).astype(o_ref.dtype)


def _in_proj(x2, g, sc, sh, tiles_per_mod, weights, out_dtypes, tm=512):
    m, d = x2.shape
    mod_map = (lambda i: (0, 0, 0)) if tiles_per_mod is None else (lambda i: (i // tiles_per_mod, 0, 0))
    in_specs = [pl.BlockSpec((tm, d), lambda i: (i, 0)), _resident((1, d)),
                pl.BlockSpec((1, 1, d), mod_map), pl.BlockSpec((1, 1, d), mod_map)]
    in_specs += [_resident(w.shape) for w in weights]
    out_specs = [pl.BlockSpec((tm, w.shape[1]), lambda i: (i, 0)) for w in weights]
    out_shape = [jax.ShapeDtypeStruct((m, w.shape[1]), dt) for w, dt in zip(weights, out_dtypes)]
    return pl.pallas_call(
        functools.partial(_in_proj_kernel, n_groups=len(weights)),
        grid=(m // tm,),
        in_specs=in_specs, out_specs=out_specs, out_shape=out_shape,
        compiler_params=_params(1),
        name="in_proj",
    )(x2, g.reshape(1, d), sc, sh, *weights)


def _na_kernel(q_ref, k_ref, v_ref, kc_ref, vc_ref, bias_ref, o_ref):
    lo = _lane() < HALF
    kc = kc_ref[...]
    vc = vc_ref[...]
    qscale = NA_HEAD_DIM ** -0.5 * LOG2E
    nq = NA_BLK * GRID_W
    nk = NA_WIN_ROWS * GRID_W
    for bi in range(NA_BLOCKS):
        r0, ws = bi * NA_BLK, _na_window_start(bi)
        q = q_ref[r0 * GRID_W:r0 * GRID_W + nq, :]
        q = (q.astype(F32) * qscale).astype(q.dtype)
        kw = k_ref[ws * GRID_W:ws * GRID_W + nk, :]
        vw = v_ref[ws * GRID_W:ws * GRID_W + nk, :]
        zero = jnp.zeros_like(q)
        q2 = jnp.concatenate([jnp.where(lo, q, zero), jnp.where(lo, zero, q)], axis=0)
        s_loc = _dot_nt(q2, kw) + bias_ref[0, _na_block_kind(bi)]
        s_ctx = _dot_nt(q2, kc)
        m = jnp.maximum(jnp.max(s_loc, axis=-1, keepdims=True), jnp.max(s_ctx, axis=-1, keepdims=True))
        p_loc = jnp.exp2(s_loc - m)
        p_ctx = jnp.exp2(s_ctx - m)
        den = jnp.sum(p_loc, axis=-1, keepdims=True) + jnp.sum(p_ctx, axis=-1, keepdims=True)
        o = (_dot(p_loc.astype(BF16), vw) + _dot(p_ctx.astype(BF16), vc)) / den
        o_ref[r0 * GRID_W:r0 * GRID_W + nq, :] = jnp.where(lo, o[:nq], o[nq:]).astype(o_ref.dtype)


def _na_window_start(bi):
    return min(max(bi * NA_BLK - NA_ROW_WIN // 2, 0), GRID_ROWS - NA_WIN_ROWS)


def _na_block_kind(bi):
    return 0 if bi == 0 else (2 if bi == NA_BLOCKS - 1 else 1)


def _na_bias_table(rpb):
    w = np.arange(GRID_W)
    j = np.arange(GRID_W)
    cs = np.clip(w - NA_COL_WIN // 2, 0, GRID_W - NA_COL_WIN)
    valid = (j[None, :] >= cs[:, None]) & (j[None, :] < cs[:, None] + NA_COL_WIN)
    col = np.clip(j[None, :] - w[:, None] + NA_COL_WIN - 1, 0, 2 * NA_COL_WIN - 2)
    n_rho = 2 * NA_ROW_WIN - 1
    t = jnp.where(valid[None, None], rpb.astype(F32)[:, :, col] * LOG2E, NEG)
    tt = t.transpose(0, 2, 1, 3).reshape(NA_HEADS, GRID_W, n_rho * GRID_W)
    ttp = jnp.pad(tt, ((0, 0), (0, 0), (NA_BLK * GRID_W, NA_BLK * GRID_W)), constant_values=NEG)
    nk = NA_WIN_ROWS * GRID_W
    patterns = {}
    for bi in range(NA_BLOCKS):
        ws = _na_window_start(bi)
        pat = []
        for u in range(NA_BLK):
            r = bi * NA_BLK + u
            rs = min(max(r - NA_ROW_WIN // 2, 0), GRID_ROWS - NA_ROW_WIN)
            rho0 = ws - r + NA_ROW_WIN - 1
            in_win = tuple(0 <= ws + i - rs < NA_ROW_WIN for i in range(NA_WIN_ROWS))
            pat.append((rho0, in_win))
        assert patterns.setdefault(_na_block_kind(bi), pat) == pat
    kinds = []
    for kind in range(3):
        per_u = []
        for rho0, in_win in patterns[kind]:
            start = (rho0 + NA_BLK) * GRID_W
            mask = np.repeat(np.asarray(in_win), GRID_W)
            per_u.append(jnp.where(mask[None, None, :], ttp[:, :, start:start + nk], NEG))
        kinds.append(jnp.stack(per_u, axis=1))
    b = jnp.stack(kinds, axis=1)
    b = b.reshape(N_PAIRS, 2, 3, NA_BLK * GRID_W, nk).transpose(0, 2, 1, 3, 4)
    return b.reshape(N_PAIRS, 3, 2 * NA_BLK * GRID_W, nk)


def _na_attention(o_na_l, o_na_c, bias, nb):
    return pl.pallas_call(
        _na_kernel,
        grid=(nb, N_PAIRS),
        in_specs=[pl.BlockSpec((SEQ, LANES), lambda b, p: (b, p)),
                  pl.BlockSpec((SEQ, LANES), lambda b, p: (b, N_PAIRS + p)),
                  pl.BlockSpec((SEQ, LANES), lambda b, p: (b, 2 * N_PAIRS + p)),
                  pl.BlockSpec((CTX_LEN, LANES), lambda b, p: (b, N_PAIRS + p)),
                  pl.BlockSpec((CTX_LEN, LANES), lambda b, p: (b, 2 * N_PAIRS + p)),
                  pl.BlockSpec((1, 3, 2 * NA_BLK * GRID_W, NA_WIN_ROWS * GRID_W), lambda b, p: (p, 0, 0, 0))],
        out_specs=pl.BlockSpec((SEQ, LANES), lambda b, p: (b, p)),
        out_shape=jax.ShapeDtypeStruct((nb * SEQ, NA_HEADS * NA_HEAD_DIM), BF16),
        compiler_params=_params(2),
        name="na_attn",
    )(o_na_l, o_na_l, o_na_l, o_na_c, o_na_c, bias)


def _attn_kernel(*refs, n_src, dq, scale):
    q_ref = refs[0]
    o_ref = refs[-1]
    lo = _lane() < HALF
    lane_o = _lane((1, QUAD_W))
    q = q_ref[...]
    if scale != 1.0:
        q = (q.astype(F32) * scale).astype(q.dtype)
    outs = []
    for a in range(QUAD):
        c0 = a * LANES if dq == LANES else (a // 2) * LANES
        qa = q[:, c0:c0 + LANES]
        if dq != LANES:
            zero = jnp.zeros_like(qa)
            qa = jnp.where(lo, qa, zero) if a % 2 == 0 else jnp.where(lo, zero, qa)
        ss = [_dot_nt(qa, refs[1 + 2 * s][:, c0:c0 + LANES]) for s in range(n_src)]
        m = ss[0].max(axis=-1, keepdims=True)
        for sc in ss[1:]:
            m = jnp.maximum(m, sc.max(axis=-1, keepdims=True))
        den = None
        o = None
        for s in range(n_src):
            p = jnp.exp2(ss[s] - m)
            ps = jnp.sum(p, axis=-1, keepdims=True)
            po = _dot(p.astype(BF16), refs[2 + 2 * s][...])
            den = ps if den is None else den + ps
            o = po if o is None else o + po
        outs.append(o / den)
    out = outs[QUAD - 1]
    for a in range(QUAD - 2, -1, -1):
        out = jnp.where(lane_o < (a + 1) * HALF, outs[a], out)
    o_ref[...] = out.astype(o_ref.dtype)


def _attention(q, srcs, *, nb, t, dq, tq, scale=1.0):
    nq = t // tq
    n_quads = 2 * N_PAIRS // QUAD
    in_specs = [pl.BlockSpec((tq, QUAD * dq), lambda b, p, i: (b * nq + i, p))]
    args = [q]
    for k_arr, v_arr, u, k_off, v_off in srcs:
        in_specs.append(pl.BlockSpec((u, QUAD * dq), lambda b, p, i, k_off=k_off: (b, k_off + p)))
        in_specs.append(pl.BlockSpec((u, QUAD_W), lambda b, p, i, v_off=v_off: (b, v_off + p)))
        args += [k_arr, v_arr]
    return pl.pallas_call(
        functools.partial(_attn_kernel, n_src=len(srcs), dq=dq, scale=scale),
        grid=(nb, n_quads, nq),
        in_specs=in_specs,
        out_specs=pl.BlockSpec((tq, QUAD_W), lambda b, p, i: (b * nq + i, p)),
        out_shape=jax.ShapeDtypeStruct((nb * t, n_quads * QUAD_W), BF16),
        compiler_params=_params(3),
        name="attn",
    )(*args)


def _mla_prep_kernel(*refs, rotate, scale):
    if rotate:
        x_ref, gq_ref, gkv_ref, wq_ref, wk_ref, wv_ref, cos_ref, sin_ref, q_ref, k_ref, v_ref = refs
    else:
        x_ref, gq_ref, gkv_ref, wq_ref, wk_ref, wv_ref, q_ref, k_ref, v_ref = refs
    x = x_ref[...].astype(F32)
    cq = x[:, :MLA_Q_RANK]
    ckv = x[:, MLA_Q_RANK:MLA_Q_RANK + MLA_KV_RANK]
    kr = x[:, MLA_Q_RANK + MLA_KV_RANK:]
    cqn = _rms(cq, gq_ref[...]).astype(BF16)
    ckvn = _rms(ckv, gkv_ref[...]).astype(BF16)
    lane = _lane()
    first_half = lane < MLA_NOPE + MLA_ROPE // 2

    def rope(t):
        if not rotate:
            return t
        rot = jnp.where(first_half, pltpu.roll(t, LANES - MLA_ROPE // 2, axis=1), pltpu.roll(t, MLA_ROPE // 2, axis=1))
        return t * cos_ref[...] + rot * sin_ref[...]

    kr = rope(kr)
    for h in range(MLA_HEADS):
        sl = slice(h * LANES, (h + 1) * LANES)
        qh = rope(_dot(cqn, wq_ref[:, sl]))
        q_ref[:, sl] = (qh * scale).astype(q_ref.dtype)
        k_ref[:, sl] = (_dot(ckvn, wk_ref[:, sl]) + kr).astype(k_ref.dtype)
    v_ref[...] = _dot(ckvn, wv_ref[...]).astype(v_ref.dtype)


def _mla_prep(o_mla, gq, gkv, wq, wk, wv, tables, tm=512):
    m = o_mla.shape[0]
    rotate = tables is not None
    scale = (MLA_NOPE + MLA_ROPE) ** -0.5 * LOG2E
    in_specs = [pl.BlockSpec((tm, o_mla.shape[1]), lambda i: (i, 0)),
                _resident((1, MLA_Q_RANK)), _resident((1, MLA_KV_RANK)),
                _resident(wq.shape), _resident(wk.shape), _resident(wv.shape)]
    args = [o_mla, gq.reshape(1, -1), gkv.reshape(1, -1), wq, wk, wv]
    if rotate:
        nt = SEQ // tm
        in_specs += [pl.BlockSpec((tm, LANES), lambda i: (i % nt, 0))] * 2
        args += list(tables)
    widths = (MLA_HEADS * LANES, MLA_HEADS * LANES, MLA_HEADS * MLA_V)
    return pl.pallas_call(
        functools.partial(_mla_prep_kernel, rotate=rotate, scale=scale),
        grid=(m // tm,),
        in_specs=in_specs,
        out_specs=[pl.BlockSpec((tm, w), lambda i: (i, 0)) for w in widths],
        out_shape=[jax.ShapeDtypeStruct((m, w), BF16) for w in widths],
        compiler_params=_params(1),
        name="mla_prep",
    )(*args)


def _gqa_prep_kernel(*refs, rotate, scale):
    if rotate:
        x_ref, gq_ref, gk_ref, bd_ref, cos_ref, sin_ref, q_ref, k_ref, v_ref = refs
    else:
        x_ref, gq_ref, gk_ref, bd_ref, q_ref, k_ref, v_ref = refs
    x = x_ref[...].astype(F32)
    lane = _lane()
    lo = lane < HALF
    first_half = (lane & (GQA_HEAD_DIM - 1)) < GQA_HEAD_DIM // 2

    def headnorm(t, g):
        ss = _dot((t * t).astype(BF16), bd_ref[...]) * (1.0 / GQA_HEAD_DIM)
        return t * lax.rsqrt(ss + EPS) * g

    def rope(t):
        if not rotate:
            return t
        rot = jnp.where(first_half, pltpu.roll(t, LANES - GQA_HEAD_DIM // 2, axis=1), pltpu.roll(t, GQA_HEAD_DIM // 2, axis=1))
        return t * cos_ref[...] + rot * sin_ref[...]

    for j in range(N_PAIRS):
        sl = slice(j * LANES, (j + 1) * LANES)
        q_ref[:, sl] = (rope(headnorm(x[:, sl], gq_ref[...])) * scale).astype(q_ref.dtype)
    nq = GQA_Q_HEADS * GQA_HEAD_DIM
    k = rope(headnorm(x[:, nq:nq + LANES], gk_ref[...]))
    v = x[:, nq + LANES:nq + 2 * LANES]
    for t, ref in ((k, k_ref), (v, v_ref)):
        sw = pltpu.roll(t, HALF, axis=1)
        for g, dup in enumerate((jnp.where(lo, t, sw), jnp.where(lo, sw, t))):
            dup = dup.astype(ref.dtype)
            ref[:, (2 * g) * LANES:(2 * g + 1) * LANES] = dup
            ref[:, (2 * g + 1) * LANES:(2 * g + 2) * LANES] = dup


def _gqa_prep(o_gqa, gq, gk, tables, tm=512):
    m = o_gqa.shape[0]
    rotate = tables is not None
    bd = jnp.asarray(np.kron(np.eye(2), np.ones((HALF, HALF))), BF16)
    in_specs = [pl.BlockSpec((tm, o_gqa.shape[1]), lambda i: (i, 0)),
                _resident((1, LANES)), _resident((1, LANES)), _resident((LANES, LANES))]
    args = [o_gqa, jnp.tile(gq, 2).reshape(1, LANES), jnp.tile(gk, 2).reshape(1, LANES), bd]
    if rotate:
        nt = SEQ // tm
        in_specs += [pl.BlockSpec((tm, LANES), lambda i: (i % nt, 0))] * 2
        args += list(tables)
    widths = (GQA_Q_HEADS * GQA_HEAD_DIM, GQA_KV_HEADS * QUAD_W, GQA_KV_HEADS * QUAD_W)
    return pl.pallas_call(
        functools.partial(_gqa_prep_kernel, rotate=rotate, scale=GQA_HEAD_DIM ** -0.5 * LOG2E),
        grid=(m // tm,),
        in_specs=in_specs,
        out_specs=[pl.BlockSpec((tm, w), lambda i: (i, 0)) for w in widths],
        out_shape=[jax.ShapeDtypeStruct((m, w), BF16) for w in widths],
        compiler_params=_params(1),
        name="gqa_prep",
    )(*args)


def _rope_tables(dim, lane_start, group):
    t = jnp.arange(SEQ)
    row = (t // GRID_W).astype(F32)
    col = (t % GRID_W).astype(F32)
    quarter = dim // 4
    inv = ROPE_THETA ** (-jnp.arange(quarter, dtype=F32) / quarter)
    ang = jnp.concatenate([row[:, None] * inv, col[:, None] * inv], axis=-1)
    cos, sin = jnp.cos(ang), jnp.sin(ang)
    cos_g = jnp.ones((SEQ, group), F32).at[:, lane_start:lane_start + dim].set(jnp.concatenate([cos, cos], axis=-1))
    sin_g = jnp.zeros((SEQ, group), F32).at[:, lane_start:lane_start + dim].set(jnp.concatenate([-sin, sin], axis=-1))
    reps = LANES // group
    return jnp.tile(cos_g, (1, reps)), jnp.tile(sin_g, (1, reps))


CONV_HALO = 16


def _conv_kernel(x_ref, prev_ref, next_ref, w_ref, b_ref, o_ref):
    i = pl.program_id(1)
    n = pl.num_programs(1)
    tm = x_ref.shape[0]
    rows = tm + 2 * CONV_HALO
    for c0 in range(0, x_ref.shape[1], LANES):
        sl = slice(c0, c0 + LANES)
        prev = jnp.where(i > 0, prev_ref[:, sl].astype(F32), 0.0)
        nxt = jnp.where(i < n - 1, next_ref[:, sl].astype(F32), 0.0)
        xcat = jnp.concatenate([prev, x_ref[:, sl].astype(F32), nxt], axis=0)
        acc = jnp.zeros((tm, LANES), F32) + b_ref[:, sl]
        for k in range(SSM_CONV):
            sh = (SSM_CONV // 2 - k) % rows
            rolled = xcat if sh == 0 else pltpu.roll(xcat, sh, axis=0)
            acc = acc + rolled[CONV_HALO:CONV_HALO + tm] * w_ref[k:k + 1, sl]
        o_ref[:, sl] = _silu(acc).astype(o_ref.dtype)


def _ssm_conv(o_ssm, conv_w, conv_b, nb, t, tm=256):
    nch = conv_w.shape[1]
    nt = t // tm
    hb = tm // CONV_HALO
    total = nb * t // CONV_HALO
    w8 = jnp.zeros((8, nch), F32).at[:SSM_CONV].set(conv_w)
    return pl.pallas_call(
        _conv_kernel,
        grid=(nb, nt),
        in_specs=[pl.BlockSpec((tm, nch), lambda b, i: (b * nt + i, 0)),
                  pl.BlockSpec((CONV_HALO, nch), lambda b, i: (jnp.maximum((b * nt + i) * hb - 1, 0), 0)),
                  pl.BlockSpec((CONV_HALO, nch), lambda b, i: (jnp.minimum((b * nt + i + 1) * hb, total - 1), 0)),
                  _resident((8, nch)), _resident((1, nch))],
        out_specs=pl.BlockSpec((tm, nch), lambda b, i: (b * nt + i, 0)),
        out_shape=jax.ShapeDtypeStruct((nb * t, nch), BF16),
        compiler_params=_params(2),
        name="ssm_conv",
    )(o_ssm, o_ssm, o_ssm, w8, conv_b.reshape(1, nch))


def _softplus(x):
    return jnp.maximum(x, 0.0) + jnp.log(1.0 + jnp.exp(-jnp.abs(x)))


def _ssd_kernel(*refs, with_y):
    n_dir = 6 if with_y else 5
    dir_refs = (refs[:n_dir], refs[n_dir:2 * n_dir])
    al_ref, alt_ref, bi_ref, bit_ref, h0_ref = refs[2 * n_dir:2 * n_dir + 5]
    if with_y:
        yf_ref, yb_ref, hf_ref, st = refs[2 * n_dir + 5:]
        y_refs = (yf_ref, yb_ref)
    else:
        hf_ref, st = refs[2 * n_dir + 5:]
    j = pl.program_id(1)
    q = SSM_CHUNK

    @pl.when(j == 0)
    def _():
        st[...] = h0_ref[:, 0]

    row = lax.broadcasted_iota(jnp.int32, (q, q), 0)
    col = lax.broadcasted_iota(jnp.int32, (q, q), 1)
    lane = _lane()
    lo = lane < HALF

    def col_of(x, h):
        return jnp.sum(jnp.where(lane == h, x, 0.0), axis=1, keepdims=True)

    for d in range(2):
        if with_y:
            x_ref, b_ref, c_ref, bt_ref, dt_ref, dtt_ref = dir_refs[d]
        else:
            x_ref, b_ref, bt_ref, dt_ref, dtt_ref = dir_refs[d]
        m_qs = (col <= row) if d == 0 else (col >= row)
        m_ts = (row <= col) if d == 0 else (row >= col)
        dt = _softplus(dt_ref[0] + bi_ref[d])
        da = jnp.where(lane < SSM_HEADS, dt * (-jnp.exp(al_ref[d])), 0.0)
        a_cum_c = jnp.dot(jnp.where(m_qs, 1.0, 0.0), da, precision=HIGHEST, preferred_element_type=F32)
        a_tot = jnp.sum(da, axis=0, keepdims=True)
        dat = _softplus(dtt_ref[0, 0] + bit_ref[d]) * (-jnp.exp(alt_ref[d]))
        a_cum_r = jnp.dot(dat, jnp.where(m_ts, 1.0, 0.0), precision=HIGHEST, preferred_element_type=F32)

        x = x_ref[...].astype(F32)
        bt = bt_ref[0]
        if with_y:
            cm = c_ref[...]
        for g in range(SSM_GROUPS):
            btg = bt[g * SSM_STATE:(g + 1) * SSM_STATE, :]
            if with_y:
                cg = cm[:, g * SSM_STATE:(g + 1) * SSM_STATE]
                cb = _dot(cg, btg)
            for pp in range(2):
                p = 2 * g + pp
                h0, h1 = 2 * p, 2 * p + 1
                ac0, ac1 = col_of(a_cum_c, h0), col_of(a_cum_c, h1)
                tot0, tot1 = col_of(a_tot, h0), col_of(a_tot, h1)
                xdt = x[:, p * LANES:(p + 1) * LANES] * jnp.where(lo, col_of(dt, h0), col_of(dt, h1))
                ht = st[d, p]
                if with_y:
                    xdt_b = xdt.astype(BF16)
                    l0 = jnp.exp(jnp.where(m_qs, ac0 - a_cum_r[h0:h0 + 1, :], NEG))
                    l1 = jnp.exp(jnp.where(m_qs, ac1 - a_cum_r[h1:h1 + 1, :], NEG))
                    y_in = jnp.where(lo, _dot((cb * l0).astype(BF16), xdt_b), _dot((cb * l1).astype(BF16), xdt_b))
                    y_st = _dot(cg, ht.astype(BF16)) * jnp.where(lo, jnp.exp(ac0), jnp.exp(ac1))
                    y_refs[d][:, p * LANES:(p + 1) * LANES] = (y_in + y_st).astype(y_refs[d].dtype)
                dec = jnp.where(lo, jnp.exp(tot0 - ac0), jnp.exp(tot1 - ac1))
                st[d, p] = ht * jnp.where(lo, jnp.exp(tot0), jnp.exp(tot1)) + _dot(btg, (xdt * dec).astype(BF16))

    @pl.when(j == pl.num_programs(1) - 1)
    def _():
        hf_ref[:, 0] = st[...]


def _ssd(xbc, b_t, dt2, dtt2, a_log, dt_bias, h0, *, nb, t, with_y):
    nc = t // SSM_CHUNK
    pad = LANES - SSM_HEADS
    al_row = jnp.pad(a_log.astype(F32), ((0, 0), (0, pad))).reshape(2, 1, LANES)
    bi_row = jnp.pad(dt_bias.astype(F32), ((0, 0), (0, pad))).reshape(2, 1, LANES)
    al_t = jnp.broadcast_to(a_log.astype(F32)[:, :, None], (2, SSM_HEADS, SSM_CHUNK))
    bi_t = jnp.broadcast_to(dt_bias.astype(F32)[:, :, None], (2, SSM_HEADS, SSM_CHUNK))
    x_blk = SSM_INNER // SSM_BC
    in_specs, args = [], []
    for d in range(2):
        cidx = (lambda j: j) if d == 0 else (lambda j: nc - 1 - j)
        in_specs += [pl.BlockSpec((SSM_CHUNK, SSM_INNER), lambda b, j, c=cidx: (b * nc + c(j), 0)),
                     pl.BlockSpec((SSM_CHUNK, SSM_BC), lambda b, j, c=cidx: (b * nc + c(j), x_blk))]
        args += [xbc, xbc]
        if with_y:
            in_specs.append(pl.BlockSpec((SSM_CHUNK, SSM_BC), lambda b, j, c=cidx: (b * nc + c(j), x_blk + 1)))
            args.append(xbc)
        in_specs += [pl.BlockSpec((1, SSM_BC, SSM_CHUNK), lambda b, j, c=cidx: (b, 0, c(j))),
                     pl.BlockSpec((1, SSM_CHUNK, LANES), lambda b, j, c=cidx, d=d: (d, b * nc + c(j), 0)),
                     pl.BlockSpec((1, 1, SSM_HEADS, SSM_CHUNK), lambda b, j, c=cidx, d=d: (d, b, 0, c(j)))]
        args += [b_t, dt2, dtt2]
    st_spec = pl.BlockSpec((2, 1, N_PAIRS, SSM_STATE, LANES), lambda b, j: (0, b, 0, 0, 0))
    in_specs += [_resident(al_row.shape), _resident(al_t.shape), _resident(bi_row.shape), _resident(bi_t.shape), st_spec]
    args += [al_row, al_t, bi_row, bi_t, h0]
    st_shape = jax.ShapeDtypeStruct((2, nb, N_PAIRS, SSM_STATE, LANES), F32)
    out_specs, out_shape = [st_spec], [st_shape]
    if with_y:
        out_specs = [pl.BlockSpec((SSM_CHUNK, SSM_INNER), lambda b, j: (b * nc + j, 0)),
                     pl.BlockSpec((SSM_CHUNK, SSM_INNER), lambda b, j: (b * nc + nc - 1 - j, 0))] + out_specs
        out_shape = [jax.ShapeDtypeStruct((nb * t, SSM_INNER), BF16)] * 2 + out_shape
    return pl.pallas_call(
        functools.partial(_ssd_kernel, with_y=with_y),
        grid=(nb, nc),
        in_specs=in_specs, out_specs=out_specs, out_shape=out_shape,
        scratch_shapes=[pltpu.VMEM((2, N_PAIRS, SSM_STATE, LANES), F32)],
        compiler_params=_params(2),
        name="ssd",
    )(*args)


def _ssm_combine_kernel(yf_ref, yb_ref, x_ref, z_ref, d_ref, g_ref, o_ref):
    y = yf_ref[...].astype(F32) + yb_ref[...].astype(F32) + d_ref[...] * x_ref[...].astype(F32)
    y = y * _silu(z_ref[...].astype(F32))
    o_ref[...] = _rms(y, g_ref[...]).astype(o_ref.dtype)


def _ssm_combine(y_f, y_b, xbc, o_ssm, d_skip, g_norm, tm=512):
    m = xbc.shape[0]
    dvec = jnp.repeat(d_skip.astype(F32), SSM_HEAD_DIM).reshape(1, SSM_INNER)
    z_blk = o_ssm.shape[1] // SSM_INNER - 1
    return pl.pallas_call(
        _ssm_combine_kernel,
        grid=(m // tm,),
        in_specs=[pl.BlockSpec((tm, SSM_INNER), lambda i: (i, 0)),
                  pl.BlockSpec((tm, SSM_INNER), lambda i: (i, 0)),
                  pl.BlockSpec((tm, SSM_INNER), lambda i: (i, 0)),
                  pl.BlockSpec((tm, SSM_INNER), lambda i: (i, z_blk)),
                  _resident((1, SSM_INNER)), _resident((1, SSM_INNER))],
        out_specs=pl.BlockSpec((tm, SSM_INNER), lambda i: (i, 0)),
        out_shape=jax.ShapeDtypeStruct((m, SSM_INNER), BF16),
        compiler_params=_params(1),
        name="ssm_combine",
    )(y_f, y_b, xbc, o_ssm, dvec, g_norm.reshape(1, SSM_INNER))


def _merge_kernel(yna_ref, ymla_ref, ygqa_ref, yssm_ref, gate_ref, wb_ref, wo_ref, x_ref, gt_ref, g_ref, o_ref):
    acc = None
    for k, y_ref in enumerate((yna_ref, ymla_ref, ygqa_ref, yssm_ref)):
        proj = _dot(y_ref[...], wb_ref[k])
        gate = jax.nn.sigmoid(gate_ref[:, k * D_MODEL:(k + 1) * D_MODEL].astype(F32))
        acc = gate * proj if acc is None else acc + gate * proj
    mix = _dot(acc.astype(BF16), wo_ref[...])
    o_ref[...] = x_ref[...] + gt_ref[0] * _rms(mix, g_ref[...])


def _merge(ys, gate, wb, wo, x2, gt, g_post, tiles_per_mod, tm=512):
    m, d = x2.shape
    mod_map = (lambda i: (0, 0, 0)) if tiles_per_mod is None else (lambda i: (i // tiles_per_mod, 0, 0))
    in_specs = [pl.BlockSpec((tm, BRANCH_W), lambda i: (i, 0))] * N_BRANCH
    in_specs += [pl.BlockSpec((tm, N_BRANCH * d), lambda i: (i, 0)),
                 _resident(wb.shape), _resident(wo.shape),
                 pl.BlockSpec((tm, d), lambda i: (i, 0)),
                 pl.BlockSpec((1, 1, d), mod_map), _resident((1, d))]
    return pl.pallas_call(
        _merge_kernel,
        grid=(m // tm,),
        in_specs=in_specs,
        out_specs=pl.BlockSpec((tm, d), lambda i: (i, 0)),
        out_shape=jax.ShapeDtypeStruct((m, d), F32),
        compiler_params=_params(1),
        name="merge",
    )(*ys, gate, wb, wo, x2, gt, g_post.reshape(1, d))


def _ffn_kernel(x_ref, gpre_ref, sc_ref, sh_ref, w1_ref, w3_ref, w2_ref, gt_ref, gpost_ref, o_ref):
    x = x_ref[...]
    hb = (_rms(x, gpre_ref[...]) * (1.0 + sc_ref[0]) + sh_ref[0]).astype(BF16)
    u = _silu(_dot(hb, w1_ref[...])) * _dot(hb, w3_ref[...])
    f = _dot(u.astype(BF16), w2_ref[...])
    o_ref[...] = x + gt_ref[0] * _rms(f, gpost_ref[...])


def _ffn(x2, g_pre, sc, sh, w1, w3, w2, gt, g_post, tiles_per_mod, tm=512):
    m, d = x2.shape
    mod_map = (lambda i: (0, 0, 0)) if tiles_per_mod is None else (lambda i: (i // tiles_per_mod, 0, 0))
    mod = pl.BlockSpec((1, 1, d), mod_map)
    return pl.pallas_call(
        _ffn_kernel,
        grid=(m // tm,),
        in_specs=[pl.BlockSpec((tm, d), lambda i: (i, 0)), _resident((1, d)), mod, mod,
                  _resident(w1.shape), _resident(w3.shape), _resident(w2.shape), mod, _resident((1, d))],
        out_specs=pl.BlockSpec((tm, d), lambda i: (i, 0)),
        out_shape=jax.ShapeDtypeStruct((m, d), F32),
        compiler_params=_params(1),
        name="ffn",
    )(x2, g_pre.reshape(1, d), sc, sh, w1, w3, w2, gt, g_post.reshape(1, d))


def _split_w_in(w_in):
    off, cols = 0, {}
    for name, w in KEY_SPLITS + QUERY_SPLITS:
        cols[name] = w_in[:, off:off + w]
        off += w
    d = w_in.shape[0]
    z = lambda n: jnp.zeros((d, n), w_in.dtype)
    cat = lambda parts: jnp.concatenate(parts, axis=1).astype(BF16)
    w_na = cat([cols['na_q'], cols['na_k'], cols['na_v']])
    w_mla = cat([cols['mla_cq'], cols['mla_ckv'], z(MLA_NOPE), cols['mla_kr'], z(LANES - MLA_NOPE - MLA_ROPE)])
    w_gqa = cat([cols['gqa_q'], cols['gqa_k'], cols['gqa_v']])
    w_ssm = cat([cols['ssm_x'], cols['ssm_B'], cols['ssm_C'], cols['ssm_z']])
    w_dt = cat([cols['ssm_dt'], z(LANES - 2 * SSM_HEADS)])
    w_gate = w_in[:, MIX_COLS:].astype(BF16)
    return w_na, w_mla, w_gqa, w_ssm, w_dt, w_gate


def _dt_layouts(o_dt, nb, t):
    both = o_dt[:, :2 * SSM_HEADS].reshape(nb * t, 2, SSM_HEADS).transpose(1, 0, 2)
    dt2 = jnp.pad(both, ((0, 0), (0, 0), (0, LANES - SSM_HEADS)))
    dtt2 = both.reshape(2, nb, t, SSM_HEADS).transpose(0, 1, 3, 2)
    return dt2, dtt2


def _layer(x2, xc2, mods, mc, lw, tables, nb, need_ctx):
    d = D_MODEL
    tpm = SEQ // 512
    chunk = lambda a, k: a[:, k * d:(k + 1) * d].reshape(-1, 1, d)
    sh1, sc1, gt1, sh2, sc2, gt2 = (chunk(mods, k) for k in range(6))
    csh1, csc1, cgt1, csh2, csc2, cgt2 = (chunk(mc, k) for k in range(6))
    w_na, w_mla, w_gqa, w_ssm, w_dt, w_gate = _split_w_in(lw['w_in'])
    dts = (BF16, BF16, BF16, BF16, F32, BF16)
    o_na, o_mla, o_gqa, o_ssm, o_dt, gate = _in_proj(
        x2, lw['g_pre1'], sc1, sh1, tpm, (w_na, w_mla, w_gqa, w_ssm, w_dt, w_gate), dts)
    if need_ctx:
        c_na, c_mla, c_gqa, c_ssm, c_dt, c_gate = _in_proj(
            xc2, lw['g_pre1'], csc1, csh1, None, (w_na, w_mla, w_gqa, w_ssm, w_dt, w_gate), dts)
    else:
        c_na, c_mla, c_gqa, c_ssm, c_dt = _in_proj(
            xc2, lw['g_pre1'], csc1, csh1, None, (w_na, w_mla, w_gqa, w_ssm, w_dt), dts[:5])

    y_na = _na_attention(o_na, c_na, _na_bias_table(lw['na_rpb']), nb)

    wq = jnp.pad(lw['mla_w_uq'].reshape(MLA_Q_RANK, MLA_HEADS, MLA_NOPE + MLA_ROPE),
                 ((0, 0), (0, 0), (0, LANES - MLA_NOPE - MLA_ROPE))).reshape(MLA_Q_RANK, MLA_HEADS * LANES).astype(BF16)
    wkv = lw['mla_w_ukv'].reshape(MLA_KV_RANK, MLA_HEADS, MLA_NOPE + MLA_V)
    wk = jnp.pad(wkv[:, :, :MLA_NOPE], ((0, 0), (0, 0), (0, LANES - MLA_NOPE))).reshape(MLA_KV_RANK, MLA_HEADS * LANES).astype(BF16)
    wv = wkv[:, :, MLA_NOPE:].reshape(MLA_KV_RANK, MLA_HEADS * MLA_V).astype(BF16)
    ql, kl, vl = _mla_prep(o_mla, lw['mla_g_q'], lw['mla_g_kv'], wq, wk, wv, tables['mla'])
    qc, kc, vc = _mla_prep(c_mla, lw['mla_g_q'], lw['mla_g_kv'], wq, wk, wv, None)
    y_mla = _attention(ql, [(kc, vc, CTX_LEN, 0, 0), (kl, vl, SEQ, 0, 0)], nb=nb, t=SEQ, dq=LANES, tq=512)

    gql, gkl, gvl = _gqa_prep(o_gqa, lw['gqa_g_q'], lw['gqa_g_k'], tables['gqa'])
    gqc, gkc, gvc = _gqa_prep(c_gqa, lw['gqa_g_q'], lw['gqa_g_k'], None)
    y_gqa = _attention(gql, [(gkc, gvc, CTX_LEN, 0, 0), (gkl, gvl, SEQ, 0, 0)], nb=nb, t=SEQ, dq=HALF, tq=512)

    nxbc = SSM_INNER + 2 * SSM_BC
    conv_w, conv_b = lw['ssm_conv_w'], lw['ssm_conv_b']
    xbc_l = _ssm_conv(o_ssm, conv_w, conv_b, nb, SEQ)
    xbc_c = _ssm_conv(c_ssm, conv_w, conv_b, nb, CTX_LEN)
    bt_of = lambda xbc, t: xbc[:, SSM_INNER:SSM_INNER + SSM_BC].reshape(nb, t, SSM_BC).transpose(0, 2, 1)
    h_zero = jnp.zeros((2, nb, N_PAIRS, SSM_STATE, LANES), F32)
    dt2c, dtt2c = _dt_layouts(c_dt, nb, CTX_LEN)
    res_c = _ssd(xbc_c, bt_of(xbc_c, CTX_LEN), dt2c, dtt2c, lw['ssm_a_log'], lw['ssm_dt_bias'], h_zero,
                 nb=nb, t=CTX_LEN, with_y=need_ctx)
    h_ctx = res_c[-1]
    dt2l, dtt2l = _dt_layouts(o_dt, nb, SEQ)
    yf_l, yb_l, _ = _ssd(xbc_l, bt_of(xbc_l, SEQ), dt2l, dtt2l, lw['ssm_a_log'], lw['ssm_dt_bias'], h_ctx,
                         nb=nb, t=SEQ, with_y=True)
    y_ssm = _ssm_combine(yf_l, yb_l, xbc_l, o_ssm, lw['ssm_d'], lw['ssm_g_norm'])

    wb = lw['w_branch'].astype(BF16)
    wo = lw['w_out'].astype(BF16)
    w1, w3, w2 = lw['ffn_w1'].astype(BF16), lw['ffn_w3'].astype(BF16), lw['ffn_w2'].astype(BF16)
    x2 = _merge((y_na, y_mla, y_gqa, y_ssm), gate, wb, wo, x2, gt1, lw['g_post1'], tpm)
    x2 = _ffn(x2, lw['g_pre2'], sc2, sh2, w1, w3, w2, gt2, lw['g_post2'], tpm)

    if need_ctx:
        yc_na = _attention(c_na, [(c_na, c_na, CTX_LEN, 2, 4)], nb=nb, t=CTX_LEN, dq=HALF,
                           tq=CTX_LEN, scale=NA_HEAD_DIM ** -0.5 * LOG2E)
        yc_mla = _attention(qc, [(kc, vc, CTX_LEN, 0, 0)], nb=nb, t=CTX_LEN, dq=LANES, tq=CTX_LEN)
        yc_gqa = _attention(gqc, [(gkc, gvc, CTX_LEN, 0, 0)], nb=nb, t=CTX_LEN, dq=HALF, tq=CTX_LEN)
        yc_ssm = _ssm_combine(res_c[0], res_c[1], xbc_c, c_ssm, lw['ssm_d'], lw['ssm_g_norm'])
        xc2 = _merge((yc_na, yc_mla, yc_gqa, yc_ssm), c_gate, wb, wo, xc2, cgt1, lw['g_post1'], None)
        xc2 = _ffn(xc2, lw['g_pre2'], csc2, csh2, w1, w3, w2, cgt2, lw['g_post2'], None)
    return x2, xc2


def kernel(x, c, ctx, c_ctx, w_ada, b_ada, g_pre1, g_post1, g_pre2, g_post2, w_in, na_rpb, mla_g_q, mla_g_kv, mla_w_uq, mla_w_ukv, gqa_g_q, gqa_g_k, ssm_conv_w, ssm_conv_b, ssm_a_log, ssm_dt_bias, ssm_d, ssm_g_norm, w_branch, w_out, ffn_w1, ffn_w3, ffn_w2):
    nb, s, d = x.shape
    assert s == SEQ and d == D_MODEL and ctx.shape[1] == CTX_LEN
    depth = w_ada.shape[0]
    tables = {'mla': _rope_tables(MLA_ROPE, MLA_NOPE, LANES), 'gqa': _rope_tables(GQA_HEAD_DIM, 0, GQA_HEAD_DIM)}
    x2 = x.reshape(nb * s, d)
    xc2 = ctx.reshape(nb * CTX_LEN, d)
    n_rows = -(-(nb + 1) // 8) * 8
    c_rows = jnp.zeros((n_rows, d), F32).at[:nb].set(c).at[nb].set(c_ctx)
    stacked = dict(w_in=w_in, na_rpb=na_rpb, mla_g_q=mla_g_q, mla_g_kv=mla_g_kv, mla_w_uq=mla_w_uq,
                   mla_w_ukv=mla_w_ukv, gqa_g_q=gqa_g_q, gqa_g_k=gqa_g_k, ssm_conv_w=ssm_conv_w,
                   ssm_conv_b=ssm_conv_b, ssm_a_log=ssm_a_log, ssm_dt_bias=ssm_dt_bias, ssm_d=ssm_d,
                   ssm_g_norm=ssm_g_norm, w_branch=w_branch, w_out=w_out, ffn_w1=ffn_w1, ffn_w3=ffn_w3,
                   ffn_w2=ffn_w2, g_pre1=g_pre1, g_post1=g_post1, g_pre2=g_pre2, g_post2=g_post2)
    for l in range(depth):
        lw = {k: v[l] for k, v in stacked.items()}
        mod_all = _ada(c_rows, w_ada[l], b_ada[l])
        x2, xc2 = _layer(x2, xc2, mod_all[:nb], mod_all[nb:nb + 1], lw, tables, nb, l < depth - 1)
    return x2.reshape(nb, s, d)
```

```python
import functools

import numpy as np
import jax
import jax.numpy as jnp
from jax import lax
from jax.experimental import pallas as pl
from jax.experimental.pallas import tpu as pltpu

F32 = jnp.float32
BF16 = jnp.bfloat16
HIGHEST = lax.Precision.HIGHEST

D_MODEL = 1024
SEQ = 2048
GRID_W = 64
GRID_ROWS = SEQ // GRID_W
CTX_LEN = 256
N_BRANCH = 4
BRANCH_W = 512
ROPE_THETA = 10000.0
EPS = 1e-6
NA_HEADS = 8
NA_HEAD_DIM = 64
NA_ROW_WIN = 8
NA_COL_WIN = 16
MLA_HEADS = 8
MLA_Q_RANK = 384
MLA_KV_RANK = 256
MLA_NOPE = 64
MLA_ROPE = 32
MLA_V = 64
GQA_Q_HEADS = 8
GQA_KV_HEADS = 2
GQA_HEAD_DIM = 64
SSM_HEADS = 8
SSM_HEAD_DIM = 64
SSM_GROUPS = 2
SSM_STATE = 128
SSM_CONV = 5
SSM_CHUNK = 128
SSM_INNER = SSM_HEADS * SSM_HEAD_DIM
SSM_BC = SSM_GROUPS * SSM_STATE
FFN_HIDDEN = ((8 * D_MODEL + 3 * 256 - 1) // (3 * 256)) * 256

KEY_SPLITS = (('na_k', 512), ('na_v', 512), ('mla_ckv', MLA_KV_RANK), ('mla_kr', MLA_ROPE),
              ('gqa_k', 128), ('gqa_v', 128), ('ssm_x', SSM_INNER), ('ssm_B', SSM_BC), ('ssm_dt', 2 * SSM_HEADS))
QUERY_SPLITS = (('na_q', 512), ('mla_cq', MLA_Q_RANK), ('gqa_q', 512), ('ssm_C', SSM_BC), ('ssm_z', SSM_INNER))
MIX_COLS = sum(w for _, w in KEY_SPLITS + QUERY_SPLITS)

LANES = 128
HALF = 64
N_PAIRS = 4
NEG = -1e30
LOG2E = 1.4426950408889634
NA_BLK = 4
NA_BLOCKS = GRID_ROWS // NA_BLK
NA_WIN_ROWS = 12
ATTN_HEADS = 8
QUAD = 4
QUAD_W = QUAD * HALF
VMEM_LIMIT = 52 * 1024 * 1024


def _params(n_axes):
    return pltpu.CompilerParams(dimension_semantics=("arbitrary",) * n_axes, vmem_limit_bytes=VMEM_LIMIT)


def _resident(shape):
    nd = len(shape)
    return pl.BlockSpec(shape, lambda *_: (0,) * nd, pipeline_mode=pl.Buffered(1))


def _dot(a, b):
    return jnp.dot(a, b, preferred_element_type=F32)


def _dot_nt(a, b):
    return lax.dot_general(a, b, (((1,), (1,)), ((), ())), preferred_element_type=F32)


def _silu(x):
    return x * jax.nn.sigmoid(x)


def _rms(x, g):
    return x * lax.rsqrt(jnp.mean(x * x, axis=-1, keepdims=True) + EPS) * g


def _lane(shape=(1, LANES)):
    return lax.broadcasted_iota(jnp.int32, shape, len(shape) - 1)


def _ada_kernel(c_ref, w_ref, b_ref, o_ref):
    a = _silu(c_ref[...]).astype(BF16)
    o_ref[...] = _dot(a, w_ref[...].astype(BF16)) + b_ref[...]


def _ada(c_rows, w_ada, b_ada):
    r, d = c_rows.shape
    n = w_ada.shape[1]
    tn = 1536
    return pl.pallas_call(
        _ada_kernel,
        grid=(n // tn,),
        in_specs=[pl.BlockSpec((r, d), lambda j: (0, 0)),
                  pl.BlockSpec((d, tn), lambda j: (0, j)),
                  pl.BlockSpec((1, tn), lambda j: (0, j))],
        out_specs=pl.BlockSpec((r, tn), lambda j: (0, j)),
        out_shape=jax.ShapeDtypeStruct((r, n), F32),
        compiler_params=_params(1),
        name="ada",
    )(c_rows, w_ada, b_ada.reshape(1, n))


def _in_proj_kernel(*refs, n_groups):
    x_ref, g_ref, sc_ref, sh_ref = refs[:4]
    w_refs = refs[4:4 + n_groups]
    o_refs = refs[4 + n_groups:]
    h = _rms(x_ref[...], g_ref[...]) * (1.0 + sc_ref[0]) + sh_ref[0]
    hb = h.astype(BF16)
    for w_ref, o_ref in zip(w_refs, o_refs):
        width = w_ref.shape[1]
        for c0 in range(0, width, 512):
            cw = min(512, width - c0)
            o_ref[:, c0:c0 + cw] = _dot(hb, w_ref[:, c0:c0 + cw]).astype(o_ref.dtype)


def _in_proj(x2, g, sc, sh, tiles_per_mod, weights, out_dtypes, tm=512):
    m, d = x2.shape
    mod_map = (lambda i: (0, 0, 0)) if tiles_per_mod is None else (lambda i: (i // tiles_per_mod, 0, 0))
    in_specs = [pl.BlockSpec((tm, d), lambda i: (i, 0)), _resident((1, d)),
                pl.BlockSpec((1, 1, d), mod_map), pl.BlockSpec((1, 1, d), mod_map)]
    in_specs += [_resident(w.shape) for w in weights]
    out_specs = [pl.BlockSpec((tm, w.shape[1]), lambda i: (i, 0)) for w in weights]
    out_shape = [jax.ShapeDtypeStruct((m, w.shape[1]), dt) for w, dt in zip(weights, out_dtypes)]
    return pl.pallas_call(
        functools.partial(_in_proj_kernel, n_groups=len(weights)),
        grid=(m // tm,),
        in_specs=in_specs, out_specs=out_specs, out_shape=out_shape,
        compiler_params=_params(1),
        name="in_proj",
    )(x2, g.reshape(1, d), sc, sh, *weights)


def _na_kernel(q_ref, k_ref, v_ref, kc_ref, vc_ref, bias_ref, o_ref):
    lo = _lane() < HALF
    kc = kc_ref[...]
    vc = vc_ref[...]
    qscale = NA_HEAD_DIM ** -0.5 * LOG2E
    nq = NA_BLK * GRID_W
    nk = NA_WIN_ROWS * GRID_W
    for bi in range(NA_BLOCKS):
        r0, ws = bi * NA_BLK, _na_window_start(bi)
        q = q_ref[r0 * GRID_W:r0 * GRID_W + nq, :]
        q = (q.astype(F32) * qscale).astype(q.dtype)
        kw = k_ref[ws * GRID_W:ws * GRID_W + nk, :]
        vw = v_ref[ws * GRID_W:ws * GRID_W + nk, :]
        zero = jnp.zeros_like(q)
        q2 = jnp.concatenate([jnp.where(lo, q, zero), jnp.where(lo, zero, q)], axis=0)
        s_loc = _dot_nt(q2, kw) + bias_ref[0, _na_block_kind(bi)]
        s_ctx = _dot_nt(q2, kc)
        m = jnp.maximum(jnp.max(s_loc, axis=-1, keepdims=True), jnp.max(s_ctx, axis=-1, keepdims=True))
        p_loc = jnp.exp2(s_loc - m)
        p_ctx = jnp.exp2(s_ctx - m)
        den = jnp.sum(p_loc, axis=-1, keepdims=True) + jnp.sum(p_ctx, axis=-1, keepdims=True)
        o = (_dot(p_loc.astype(BF16), vw) + _dot(p_ctx.astype(BF16), vc)) / den
        o_ref[r0 * GRID_W:r0 * GRID_W + nq, :] = jnp.where(lo, o[:nq], o[nq:]).astype(o_ref.dtype)


def _na_window_start(bi):
    return min(max(bi * NA_BLK - NA_ROW_WIN // 2, 0), GRID_ROWS - NA_WIN_ROWS)


def _na_block_kind(bi):
    return 0 if bi == 0 else (2 if bi == NA_BLOCKS - 1 else 1)


def _na_bias_table(rpb):
    w = np.arange(GRID_W)
    j = np.arange(GRID_W)
    cs = np.clip(w - NA_COL_WIN // 2, 0, GRID_W - NA_COL_WIN)
    valid = (j[None, :] >= cs[:, None]) & (j[None, :] < cs[:, None] + NA_COL_WIN)
    col = np.clip(j[None, :] - w[:, None] + NA_COL_WIN - 1, 0, 2 * NA_COL_WIN - 2)
    n_rho = 2 * NA_ROW_WIN - 1
    t = jnp.where(valid[None, None], rpb.astype(F32)[:, :, col] * LOG2E, NEG)
    tt = t.transpose(0, 2, 1, 3).reshape(NA_HEADS, GRID_W, n_rho * GRID_W)
    ttp = jnp.pad(tt, ((0, 0), (0, 0), (NA_BLK * GRID_W, NA_BLK * GRID_W)), constant_values=NEG)
    nk = NA_WIN_ROWS * GRID_W
    patterns = {}
    for bi in range(NA_BLOCKS):
        ws = _na_window_start(bi)
        pat = []
        for u in range(NA_BLK):
            r = bi * NA_BLK + u
            rs = min(max(r - NA_ROW_WIN // 2, 0), GRID_ROWS - NA_ROW_WIN)
            rho0 = ws - r + NA_ROW_WIN - 1
            in_win = tuple(0 <= ws + i - rs < NA_ROW_WIN for i in range(NA_WIN_ROWS))
            pat.append((rho0, in_win))
        assert patterns.setdefault(_na_block_kind(bi), pat) == pat
    kinds = []
    for kind in range(3):
        per_u = []
        for rho0, in_win in patterns[kind]:
            start = (rho0 + NA_BLK) * GRID_W
            mask = np.repeat(np.asarray(in_win), GRID_W)
            per_u.append(jnp.where(mask[None, None, :], ttp[:, :, start:start + nk], NEG))
        kinds.append(jnp.stack(per_u, axis=1))
    b = jnp.stack(kinds, axis=1)
    b = b.reshape(N_PAIRS, 2, 3, NA_BLK * GRID_W, nk).transpose(0, 2, 1, 3, 4)
    return b.reshape(N_PAIRS, 3, 2 * NA_BLK * GRID_W, nk)


def _na_attention(o_na_l, o_na_c, bias, nb):
    return pl.pallas_call(
        _na_kernel,
        grid=(nb, N_PAIRS),
        in_specs=[pl.BlockSpec((SEQ, LANES), lambda b, p: (b, p)),
                  pl.BlockSpec((SEQ, LANES), lambda b, p: (b, N_PAIRS + p)),
                  pl.BlockSpec((SEQ, LANES), lambda b, p: (b, 2 * N_PAIRS + p)),
                  pl.BlockSpec((CTX_LEN, LANES), lambda b, p: (b, N_PAIRS + p)),
                  pl.BlockSpec((CTX_LEN, LANES), lambda b, p: (b, 2 * N_PAIRS + p)),
                  pl.BlockSpec((1, 3, 2 * NA_BLK * GRID_W, NA_WIN_ROWS * GRID_W), lambda b, p: (p, 0, 0, 0))],
        out_specs=pl.BlockSpec((SEQ, LANES), lambda b, p: (b, p)),
        out_shape=jax.ShapeDtypeStruct((nb * SEQ, NA_HEADS * NA_HEAD_DIM), BF16),
        compiler_params=_params(2),
        name="na_attn",
    )(o_na_l, o_na_l, o_na_l, o_na_c, o_na_c, bias)


def _attn_kernel(*refs, n_src, dq, scale):
    q_ref = refs[0]
    o_ref = refs[-1]
    lo = _lane() < HALF
    lane_o = _lane((1, QUAD_W))
    for quad in range(ATTN_HEADS // QUAD):
        vsl = slice(quad * QUAD_W, (quad + 1) * QUAD_W)
        outs = []
        for a in range(quad * QUAD, (quad + 1) * QUAD):
            c0 = a * LANES if dq == LANES else (a // 2) * LANES
            qa = q_ref[:, c0:c0 + LANES]
            if scale != 1.0:
                qa = (qa.astype(F32) * scale).astype(qa.dtype)
            if dq != LANES:
                zero = jnp.zeros_like(qa)
                qa = jnp.where(lo, qa, zero) if a % 2 == 0 else jnp.where(lo, zero, qa)
            ss = [_dot_nt(qa, refs[1 + 2 * s][:, c0:c0 + LANES]) for s in range(n_src)]
            m = ss[0].max(axis=-1, keepdims=True)
            for sc in ss[1:]:
                m = jnp.maximum(m, sc.max(axis=-1, keepdims=True))
            den = None
            o = None
            for s in range(n_src):
                p = jnp.exp2(ss[s] - m)
                ps = jnp.sum(p, axis=-1, keepdims=True)
                po = _dot(p.astype(BF16), refs[2 + 2 * s][:, vsl])
                den = ps if den is None else den + ps
                o = po if o is None else o + po
            outs.append(o / den)
        out = outs[QUAD - 1]
        for a in range(QUAD - 2, -1, -1):
            out = jnp.where(lane_o < (a + 1) * HALF, outs[a], out)
        o_ref[:, vsl] = out.astype(o_ref.dtype)


def _attention(q, srcs, *, nb, t, dq, tq, scale=1.0):
    nq = t // tq
    in_specs = [pl.BlockSpec((tq, ATTN_HEADS * dq), lambda b, i: (b * nq + i, 0))]
    args = [q]
    for k_arr, v_arr, u, k_off, v_off in srcs:
        in_specs.append(pl.BlockSpec((u, ATTN_HEADS * dq), lambda b, i, k_off=k_off: (b, k_off)))
        in_specs.append(pl.BlockSpec((u, ATTN_HEADS * HALF), lambda b, i, v_off=v_off: (b, v_off)))
        args += [k_arr, v_arr]
    return pl.pallas_call(
        functools.partial(_attn_kernel, n_src=len(srcs), dq=dq, scale=scale),
        grid=(nb, nq),
        in_specs=in_specs,
        out_specs=pl.BlockSpec((tq, ATTN_HEADS * HALF), lambda b, i: (b * nq + i, 0)),
        out_shape=jax.ShapeDtypeStruct((nb * t, ATTN_HEADS * HALF), BF16),
        compiler_params=_params(2),
        name="attn",
    )(*args)


def _mla_prep_kernel(*refs, rotate, scale):
    if rotate:
        x_ref, gq_ref, gkv_ref, wq_ref, wk_ref, wv_ref, cos_ref, sin_ref, q_ref, k_ref, v_ref = refs
    else:
        x_ref, gq_ref, gkv_ref, wq_ref, wk_ref, wv_ref, q_ref, k_ref, v_ref = refs
    x = x_ref[...].astype(F32)
    cq = x[:, :MLA_Q_RANK]
    ckv = x[:, MLA_Q_RANK:MLA_Q_RANK + MLA_KV_RANK]
    kr = x[:, MLA_Q_RANK + MLA_KV_RANK:]
    cqn = _rms(cq, gq_ref[...]).astype(BF16)
    ckvn = _rms(ckv, gkv_ref[...]).astype(BF16)
    lane = _lane()
    first_half = lane < MLA_NOPE + MLA_ROPE // 2

    def rope(t):
        if not rotate:
            return t
        rot = jnp.where(first_half, pltpu.roll(t, LANES - MLA_ROPE // 2, axis=1), pltpu.roll(t, MLA_ROPE // 2, axis=1))
        return t * cos_ref[...] + rot * sin_ref[...]

    kr = rope(kr)
    for h in range(MLA_HEADS):
        sl = slice(h * LANES, (h + 1) * LANES)
        qh = rope(_dot(cqn, wq_ref[:, sl]))
        q_ref[:, sl] = (qh * scale).astype(q_ref.dtype)
        k_ref[:, sl] = (_dot(ckvn, wk_ref[:, sl]) + kr).astype(k_ref.dtype)
    v_ref[...] = _dot(ckvn, wv_ref[...]).astype(v_ref.dtype)


def _mla_prep(o_mla, gq, gkv, wq, wk, wv, tables, tm=512):
    m = o_mla.shape[0]
    rotate = tables is not None
    scale = (MLA_NOPE + MLA_ROPE) ** -0.5 * LOG2E
    in_specs = [pl.BlockSpec((tm, o_mla.shape[1]), lambda i: (i, 0)),
                _resident((1, MLA_Q_RANK)), _resident((1, MLA_KV_RANK)),
                _resident(wq.shape), _resident(wk.shape), _resident(wv.shape)]
    args = [o_mla, gq.reshape(1, -1), gkv.reshape(1, -1), wq, wk, wv]
    if rotate:
        nt = SEQ // tm
        in_specs += [pl.BlockSpec((tm, LANES), lambda i: (i % nt, 0))] * 2
        args += list(tables)
    widths = (MLA_HEADS * LANES, MLA_HEADS * LANES, MLA_HEADS * MLA_V)
    return pl.pallas_call(
        functools.partial(_mla_prep_kernel, rotate=rotate, scale=scale),
        grid=(m // tm,),
        in_specs=in_specs,
        out_specs=[pl.BlockSpec((tm, w), lambda i: (i, 0)) for w in widths],
        out_shape=[jax.ShapeDtypeStruct((m, w), BF16) for w in widths],
        compiler_params=_params(1),
        name="mla_prep",
    )(*args)


def _gqa_prep_kernel(*refs, rotate, scale):
    if rotate:
        x_ref, gq_ref, gk_ref, bd_ref, cos_ref, sin_ref, q_ref, k_ref, v_ref = refs
    else:
        x_ref, gq_ref, gk_ref, bd_ref, q_ref, k_ref, v_ref = refs
    x = x_ref[...].astype(F32)
    lane = _lane()
    lo = lane < HALF
    first_half = (lane & (GQA_HEAD_DIM - 1)) < GQA_HEAD_DIM // 2

    def headnorm(t, g):
        ss = _dot((t * t).astype(BF16), bd_ref[...]) * (1.0 / GQA_HEAD_DIM)
        return t * lax.rsqrt(ss + EPS) * g

    def rope(t):
        if not rotate:
            return t
        rot = jnp.where(first_half, pltpu.roll(t, LANES - GQA_HEAD_DIM // 2, axis=1), pltpu.roll(t, GQA_HEAD_DIM // 2, axis=1))
        return t * cos_ref[...] + rot * sin_ref[...]

    for j in range(N_PAIRS):
        sl = slice(j * LANES, (j + 1) * LANES)
        q_ref[:, sl] = (rope(headnorm(x[:, sl], gq_ref[...])) * scale).astype(q_ref.dtype)
    nq = GQA_Q_HEADS * GQA_HEAD_DIM
    k = rope(headnorm(x[:, nq:nq + LANES], gk_ref[...]))
    v = x[:, nq + LANES:nq + 2 * LANES]
    for t, ref in ((k, k_ref), (v, v_ref)):
        sw = pltpu.roll(t, HALF, axis=1)
        for g, dup in enumerate((jnp.where(lo, t, sw), jnp.where(lo, sw, t))):
            dup = dup.astype(ref.dtype)
            ref[:, (2 * g) * LANES:(2 * g + 1) * LANES] = dup
            ref[:, (2 * g + 1) * LANES:(2 * g + 2) * LANES] = dup


def _gqa_prep(o_gqa, gq, gk, tables, tm=512):
    m = o_gqa.shape[0]
    rotate = tables is not None
    bd = jnp.asarray(np.kron(np.eye(2), np.ones((HALF, HALF))), BF16)
    in_specs = [pl.BlockSpec((tm, o_gqa.shape[1]), lambda i: (i, 0)),
                _resident((1, LANES)), _resident((1, LANES)), _resident((LANES, LANES))]
    args = [o_gqa, jnp.tile(gq, 2).reshape(1, LANES), jnp.tile(gk, 2).reshape(1, LANES), bd]
    if rotate:
        nt = SEQ // tm
        in_specs += [pl.BlockSpec((tm, LANES), lambda i: (i % nt, 0))] * 2
        args += list(tables)
    widths = (GQA_Q_HEADS * GQA_HEAD_DIM, GQA_KV_HEADS * QUAD_W, GQA_KV_HEADS * QUAD_W)
    return pl.pallas_call(
        functools.partial(_gqa_prep_kernel, rotate=rotate, scale=GQA_HEAD_DIM ** -0.5 * LOG2E),
        grid=(m // tm,),
        in_specs=in_specs,
        out_specs=[pl.BlockSpec((tm, w), lambda i: (i, 0)) for w in widths],
        out_shape=[jax.ShapeDtypeStruct((m, w), BF16) for w in widths],
        compiler_params=_params(1),
        name="gqa_prep",
    )(*args)


def _rope_tables(dim, lane_start, group):
    t = jnp.arange(SEQ)
    row = (t // GRID_W).astype(F32)
    col = (t % GRID_W).astype(F32)
    quarter = dim // 4
    inv = ROPE_THETA ** (-jnp.arange(quarter, dtype=F32) / quarter)
    ang = jnp.concatenate([row[:, None] * inv, col[:, None] * inv], axis=-1)
    cos, sin = jnp.cos(ang), jnp.sin(ang)
    cos_g = jnp.ones((SEQ, group), F32).at[:, lane_start:lane_start + dim].set(jnp.concatenate([cos, cos], axis=-1))
    sin_g = jnp.zeros((SEQ, group), F32).at[:, lane_start:lane_start + dim].set(jnp.concatenate([-sin, sin], axis=-1))
    reps = LANES // group
    return jnp.tile(cos_g, (1, reps)), jnp.tile(sin_g, (1, reps))


CONV_HALO = 16


def _conv_kernel(x_ref, prev_ref, next_ref, w_ref, b_ref, o_ref):
    i = pl.program_id(1)
    n = pl.num_programs(1)
    tm = x_ref.shape[0]
    rows = tm + 2 * CONV_HALO
    for c0 in range(0, x_ref.shape[1], LANES):
        sl = slice(c0, c0 + LANES)
        prev = jnp.where(i > 0, prev_ref[:, sl].astype(F32), 0.0)
        nxt = jnp.where(i < n - 1, next_ref[:, sl].astype(F32), 0.0)
        xcat = jnp.concatenate([prev, x_ref[:, sl].astype(F32), nxt], axis=0)
        acc = jnp.zeros((tm, LANES), F32) + b_ref[:, sl]
        for k in range(SSM_CONV):
            sh = (SSM_CONV // 2 - k) % rows
            rolled = xcat if sh == 0 else pltpu.roll(xcat, sh, axis=0)
            acc = acc + rolled[CONV_HALO:CONV_HALO + tm] * w_ref[k:k + 1, sl]
        o_ref[:, sl] = _silu(acc).astype(o_ref.dtype)


def _ssm_conv(o_ssm, conv_w, conv_b, nb, t, tm=256):
    nch = conv_w.shape[1]
    nt = t // tm
    hb = tm // CONV_HALO
    total = nb * t // CONV_HALO
    w8 = jnp.zeros((8, nch), F32).at[:SSM_CONV].set(conv_w)
    return pl.pallas_call(
        _conv_kernel,
        grid=(nb, nt),
        in_specs=[pl.BlockSpec((tm, nch), lambda b, i: (b * nt + i, 0)),
                  pl.BlockSpec((CONV_HALO, nch), lambda b, i: (jnp.maximum((b * nt + i) * hb - 1, 0), 0)),
                  pl.BlockSpec((CONV_HALO, nch), lambda b, i: (jnp.minimum((b * nt + i + 1) * hb, total - 1), 0)),
                  _resident((8, nch)), _resident((1, nch))],
        out_specs=pl.BlockSpec((tm, nch), lambda b, i: (b * nt + i, 0)),
        out_shape=jax.ShapeDtypeStruct((nb * t, nch), BF16),
        compiler_params=_params(2),
        name="ssm_conv",
    )(o_ssm, o_ssm, o_ssm, w8, conv_b.reshape(1, nch))


def _softplus(x):
    return jnp.maximum(x, 0.0) + jnp.log(1.0 + jnp.exp(-jnp.abs(x)))


def _ssd_kernel(*refs, with_y):
    n_dir = 5 if with_y else 4
    dir_refs = (refs[:n_dir], refs[n_dir:2 * n_dir])
    al_ref, alt_ref, bi_ref, bit_ref, h0_ref = refs[2 * n_dir:2 * n_dir + 5]
    if with_y:
        yf_ref, yb_ref, hf_ref, st = refs[2 * n_dir + 5:]
        y_refs = (yf_ref, yb_ref)
    else:
        hf_ref, st = refs[2 * n_dir + 5:]
    j = pl.program_id(1)
    q = SSM_CHUNK

    @pl.when(j == 0)
    def _():
        st[...] = h0_ref[:, 0]

    row = lax.broadcasted_iota(jnp.int32, (q, q), 0)
    col = lax.broadcasted_iota(jnp.int32, (q, q), 1)
    lane = _lane()
    lo = lane < HALF

    def col_of(x, h):
        return jnp.sum(jnp.where(lane == h, x, 0.0), axis=1, keepdims=True)

    for d in range(2):
        if with_y:
            x_ref, c_ref, bt_ref, dt_ref, dtt_ref = dir_refs[d]
        else:
            x_ref, bt_ref, dt_ref, dtt_ref = dir_refs[d]
        m_qs = (col <= row) if d == 0 else (col >= row)
        m_ts = (row <= col) if d == 0 else (row >= col)
        dt = _softplus(dt_ref[...] + bi_ref[d])
        da = jnp.where(lane < SSM_HEADS, dt * (-jnp.exp(al_ref[d])), 0.0)
        a_cum_c = jnp.dot(jnp.where(m_qs, 1.0, 0.0), da, precision=HIGHEST, preferred_element_type=F32)
        a_tot = jnp.sum(da, axis=0, keepdims=True)
        dat = _softplus(dtt_ref[0, 0] + bit_ref[d]) * (-jnp.exp(alt_ref[d]))
        a_cum_r = jnp.dot(dat, jnp.where(m_ts, 1.0, 0.0), precision=HIGHEST, preferred_element_type=F32)

        x = x_ref[...].astype(F32)
        bt = bt_ref[0]
        if with_y:
            cm = c_ref[...]
        for g in range(SSM_GROUPS):
            btg = bt[g * SSM_STATE:(g + 1) * SSM_STATE, :]
            if with_y:
                cg = cm[:, g * SSM_STATE:(g + 1) * SSM_STATE]
                cb = _dot(cg, btg)
            for pp in range(2):
                p = 2 * g + pp
                h0, h1 = 2 * p, 2 * p + 1
                ac0, ac1 = col_of(a_cum_c, h0), col_of(a_cum_c, h1)
                tot0, tot1 = col_of(a_tot, h0), col_of(a_tot, h1)
                xdt = x[:, p * LANES:(p + 1) * LANES] * jnp.where(lo, col_of(dt, h0), col_of(dt, h1))
                ht = st[d, p]
                if with_y:
                    xdt_b = xdt.astype(BF16)
                    l0 = jnp.exp(jnp.where(m_qs, ac0 - a_cum_r[h0:h0 + 1, :], NEG))
                    l1 = jnp.exp(jnp.where(m_qs, ac1 - a_cum_r[h1:h1 + 1, :], NEG))
                    y_in = jnp.where(lo, _dot((cb * l0).astype(BF16), xdt_b), _dot((cb * l1).astype(BF16), xdt_b))
                    y_st = _dot(cg, ht.astype(BF16)) * jnp.where(lo, jnp.exp(ac0), jnp.exp(ac1))
                    y_refs[d][:, p * LANES:(p + 1) * LANES] = (y_in + y_st).astype(y_refs[d].dtype)
                dec = jnp.where(lo, jnp.exp(tot0 - ac0), jnp.exp(tot1 - ac1))
                st[d, p] = ht * jnp.where(lo, jnp.exp(tot0), jnp.exp(tot1)) + _dot(btg, (xdt * dec).astype(BF16))

    @pl.when(j == pl.num_programs(1) - 1)
    def _():
        hf_ref[:, 0] = st[...]


def _ssd(xbc, b_t, o_dt, a_log, dt_bias, h0, *, nb, t, with_y):
    nc = t // SSM_CHUNK
    dtt2 = _dt_transposed(o_dt, nb, t)
    pad = LANES - SSM_HEADS
    al_row = jnp.pad(a_log.astype(F32), ((0, 0), (0, pad))).reshape(2, 1, LANES)
    bi_row = jnp.pad(dt_bias.astype(F32), ((0, 0), (0, pad))).reshape(2, 1, LANES)
    al_t = jnp.broadcast_to(a_log.astype(F32)[:, :, None], (2, SSM_HEADS, SSM_CHUNK))
    bi_t = jnp.broadcast_to(dt_bias.astype(F32)[:, :, None], (2, SSM_HEADS, SSM_CHUNK))
    x_blk = SSM_INNER // SSM_BC
    in_specs, args = [], []
    for d in range(2):
        cidx = (lambda j: j) if d == 0 else (lambda j: nc - 1 - j)
        in_specs.append(pl.BlockSpec((SSM_CHUNK, SSM_INNER), lambda b, j, c=cidx: (b * nc + c(j), 0)))
        args.append(xbc)
        if with_y:
            in_specs.append(pl.BlockSpec((SSM_CHUNK, SSM_BC), lambda b, j, c=cidx: (b * nc + c(j), x_blk + 1)))
            args.append(xbc)
        in_specs += [pl.BlockSpec((1, SSM_BC, SSM_CHUNK), lambda b, j, c=cidx: (b, 0, c(j))),
                     pl.BlockSpec((SSM_CHUNK, LANES), lambda b, j, c=cidx, d=d: (b * nc + c(j), d)),
                     pl.BlockSpec((1, 1, SSM_HEADS, SSM_CHUNK), lambda b, j, c=cidx, d=d: (d, b, 0, c(j)))]
        args += [b_t, o_dt, dtt2]
    st_spec = pl.BlockSpec((2, 1, N_PAIRS, SSM_STATE, LANES), lambda b, j: (0, b, 0, 0, 0))
    in_specs += [_resident(al_row.shape), _resident(al_t.shape), _resident(bi_row.shape), _resident(bi_t.shape), st_spec]
    args += [al_row, al_t, bi_row, bi_t, h0]
    st_shape = jax.ShapeDtypeStruct((2, nb, N_PAIRS, SSM_STATE, LANES), F32)
    out_specs, out_shape = [st_spec], [st_shape]
    if with_y:
        out_specs = [pl.BlockSpec((SSM_CHUNK, SSM_INNER), lambda b, j: (b * nc + j, 0)),
                     pl.BlockSpec((SSM_CHUNK, SSM_INNER), lambda b, j: (b * nc + nc - 1 - j, 0))] + out_specs
        out_shape = [jax.ShapeDtypeStruct((nb * t, SSM_INNER), BF16)] * 2 + out_shape
    return pl.pallas_call(
        functools.partial(_ssd_kernel, with_y=with_y),
        grid=(nb, nc),
        in_specs=in_specs, out_specs=out_specs, out_shape=out_shape,
        scratch_shapes=[pltpu.VMEM((2, N_PAIRS, SSM_STATE, LANES), F32)],
        compiler_params=_params(2),
        name="ssd",
    )(*args)


def _merge_kernel(yna_ref, ymla_ref, ygqa_ref, yf_ref, yb_ref, xs_ref, z_ref, dsk_ref, gssm_ref,
                  gate_ref, wb_ref, wo_ref, x_ref, gt_ref, g_ref, o_ref):
    y = yf_ref[...].astype(F32) + yb_ref[...].astype(F32) + dsk_ref[...] * xs_ref[...].astype(F32)
    y_ssm = _rms(y * _silu(z_ref[...].astype(F32)), gssm_ref[...]).astype(BF16)
    acc = None
    for k, yk in enumerate((yna_ref[...], ymla_ref[...], ygqa_ref[...], y_ssm)):
        proj = _dot(yk, wb_ref[k])
        gate = jax.nn.sigmoid(gate_ref[:, k * D_MODEL:(k + 1) * D_MODEL].astype(F32))
        acc = gate * proj if acc is None else acc + gate * proj
    mix = _dot(acc.astype(BF16), wo_ref[...])
    o_ref[...] = x_ref[...] + gt_ref[0] * _rms(mix, g_ref[...])


def _merge(ys, y_f, y_b, xbc, o_ssm, d_skip, g_norm, gate, wb, wo, x2, gt, g_post, tiles_per_mod, tm=512):
    m, d = x2.shape
    mod_map = (lambda i: (0, 0, 0)) if tiles_per_mod is None else (lambda i: (i // tiles_per_mod, 0, 0))
    dvec = jnp.repeat(d_skip.astype(F32), SSM_HEAD_DIM).reshape(1, SSM_INNER)
    z_blk = o_ssm.shape[1] // SSM_INNER - 1
    row_blk = pl.BlockSpec((tm, BRANCH_W), lambda i: (i, 0))
    in_specs = [row_blk] * (N_BRANCH - 1) + [row_blk, row_blk, row_blk,
                                             pl.BlockSpec((tm, SSM_INNER), lambda i: (i, z_blk)),
                                             _resident((1, SSM_INNER)), _resident((1, SSM_INNER))]
    in_specs += [pl.BlockSpec((tm, N_BRANCH * d), lambda i: (i, 0)),
                 _resident(wb.shape), _resident(wo.shape),
                 pl.BlockSpec((tm, d), lambda i: (i, 0)),
                 pl.BlockSpec((1, 1, d), mod_map), _resident((1, d))]
    return pl.pallas_call(
        _merge_kernel,
        grid=(m // tm,),
        in_specs=in_specs,
        out_specs=pl.BlockSpec((tm, d), lambda i: (i, 0)),
        out_shape=jax.ShapeDtypeStruct((m, d), F32),
        compiler_params=_params(1),
        name="merge",
    )(*ys, y_f, y_b, xbc, o_ssm, dvec, g_norm.reshape(1, SSM_INNER), gate, wb, wo, x2, gt, g_post.reshape(1, d))


def _ffn_kernel(x_ref, gpre_ref, sc_ref, sh_ref, w1_ref, w3_ref, w2_ref, gt_ref, gpost_ref, o_ref):
    x = x_ref[...]
    hb = (_rms(x, gpre_ref[...]) * (1.0 + sc_ref[0]) + sh_ref[0]).astype(BF16)
    u = _silu(_dot(hb, w1_ref[...])) * _dot(hb, w3_ref[...])
    f = _dot(u.astype(BF16), w2_ref[...])
    o_ref[...] = x + gt_ref[0] * _rms(f, gpost_ref[...])


def _ffn(x2, g_pre, sc, sh, w1, w3, w2, gt, g_post, tiles_per_mod, tm=512):
    m, d = x2.shape
    mod_map = (lambda i: (0, 0, 0)) if tiles_per_mod is None else (lambda i: (i // tiles_per_mod, 0, 0))
    mod = pl.BlockSpec((1, 1, d), mod_map)
    return pl.pallas_call(
        _ffn_kernel,
        grid=(m // tm,),
        in_specs=[pl.BlockSpec((tm, d), lambda i: (i, 0)), _resident((1, d)), mod, mod,
                  _resident(w1.shape), _resident(w3.shape), _resident(w2.shape), mod, _resident((1, d))],
        out_specs=pl.BlockSpec((tm, d), lambda i: (i, 0)),
        out_shape=jax.ShapeDtypeStruct((m, d), F32),
        compiler_params=_params(1),
        name="ffn",
    )(x2, g_pre.reshape(1, d), sc, sh, w1, w3, w2, gt, g_post.reshape(1, d))


def _split_w_in(w_in):
    off, cols = 0, {}
    for name, w in KEY_SPLITS + QUERY_SPLITS:
        cols[name] = w_in[:, off:off + w]
        off += w
    d = w_in.shape[0]
    z = lambda n: jnp.zeros((d, n), w_in.dtype)
    cat = lambda parts: jnp.concatenate(parts, axis=1).astype(BF16)
    w_na = cat([cols['na_q'], cols['na_k'], cols['na_v']])
    w_mla = cat([cols['mla_cq'], cols['mla_ckv'], z(MLA_NOPE), cols['mla_kr'], z(LANES - MLA_NOPE - MLA_ROPE)])
    w_gqa = cat([cols['gqa_q'], cols['gqa_k'], cols['gqa_v']])
    w_ssm = cat([cols['ssm_x'], cols['ssm_B'], cols['ssm_C'], cols['ssm_z']])
    w_dt = cat([cols['ssm_dt'][:, :SSM_HEADS], z(LANES - SSM_HEADS), cols['ssm_dt'][:, SSM_HEADS:], z(LANES - SSM_HEADS)])
    w_gate = w_in[:, MIX_COLS:].astype(BF16)
    return w_na, w_mla, w_gqa, w_ssm, w_dt, w_gate


def _dt_transposed(o_dt, nb, t):
    both = o_dt.reshape(nb, t, 2, LANES)[:, :, :, :SSM_HEADS]
    return both.transpose(2, 0, 3, 1)


def _layer(x2, xc2, mods, mc, lw, tables, nb, need_ctx):
    d = D_MODEL
    tpm = SEQ // 512
    chunk = lambda a, k: a[:, k * d:(k + 1) * d].reshape(-1, 1, d)
    sh1, sc1, gt1, sh2, sc2, gt2 = (chunk(mods, k) for k in range(6))
    csh1, csc1, cgt1, csh2, csc2, cgt2 = (chunk(mc, k) for k in range(6))
    w_na, w_mla, w_gqa, w_ssm, w_dt, w_gate = _split_w_in(lw['w_in'])
    dts = (BF16, BF16, BF16, BF16, F32, BF16)
    o_na, o_mla, o_gqa, o_ssm, o_dt, gate = _in_proj(
        x2, lw['g_pre1'], sc1, sh1, tpm, (w_na, w_mla, w_gqa, w_ssm, w_dt, w_gate), dts)
    if need_ctx:
        c_na, c_mla, c_gqa, c_ssm, c_dt, c_gate = _in_proj(
            xc2, lw['g_pre1'], csc1, csh1, None, (w_na, w_mla, w_gqa, w_ssm, w_dt, w_gate), dts)
    else:
        c_na, c_mla, c_gqa, c_ssm, c_dt = _in_proj(
            xc2, lw['g_pre1'], csc1, csh1, None, (w_na, w_mla, w_gqa, w_ssm, w_dt), dts[:5])

    y_na = _na_attention(o_na, c_na, _na_bias_table(lw['na_rpb']), nb)

    wq = jnp.pad(lw['mla_w_uq'].reshape(MLA_Q_RANK, MLA_HEADS, MLA_NOPE + MLA_ROPE),
                 ((0, 0), (0, 0), (0, LANES - MLA_NOPE - MLA_ROPE))).reshape(MLA_Q_RANK, MLA_HEADS * LANES).astype(BF16)
    wkv = lw['mla_w_ukv'].reshape(MLA_KV_RANK, MLA_HEADS, MLA_NOPE + MLA_V)
    wk = jnp.pad(wkv[:, :, :MLA_NOPE], ((0, 0), (0, 0), (0, LANES - MLA_NOPE))).reshape(MLA_KV_RANK, MLA_HEADS * LANES).astype(BF16)
    wv = wkv[:, :, MLA_NOPE:].reshape(MLA_KV_RANK, MLA_HEADS * MLA_V).astype(BF16)
    ql, kl, vl = _mla_prep(o_mla, lw['mla_g_q'], lw['mla_g_kv'], wq, wk, wv, tables['mla'])
    qc, kc, vc = _mla_prep(c_mla, lw['mla_g_q'], lw['mla_g_kv'], wq, wk, wv, None)
    y_mla = _attention(ql, [(kc, vc, CTX_LEN, 0, 0), (kl, vl, SEQ, 0, 0)], nb=nb, t=SEQ, dq=LANES, tq=256)

    gql, gkl, gvl = _gqa_prep(o_gqa, lw['gqa_g_q'], lw['gqa_g_k'], tables['gqa'])
    gqc, gkc, gvc = _gqa_prep(c_gqa, lw['gqa_g_q'], lw['gqa_g_k'], None)
    y_gqa = _attention(gql, [(gkc, gvc, CTX_LEN, 0, 0), (gkl, gvl, SEQ, 0, 0)], nb=nb, t=SEQ, dq=HALF, tq=256)

    conv_w, conv_b = lw['ssm_conv_w'], lw['ssm_conv_b']
    xbc_l = _ssm_conv(o_ssm, conv_w, conv_b, nb, SEQ)
    xbc_c = _ssm_conv(c_ssm, conv_w, conv_b, nb, CTX_LEN)
    bt_of = lambda xbc, t: xbc[:, SSM_INNER:SSM_INNER + SSM_BC].reshape(nb, t, SSM_BC).transpose(0, 2, 1)
    h_zero = jnp.zeros((2, nb, N_PAIRS, SSM_STATE, LANES), F32)
    res_c = _ssd(xbc_c, bt_of(xbc_c, CTX_LEN), c_dt, lw['ssm_a_log'], lw['ssm_dt_bias'], h_zero,
                 nb=nb, t=CTX_LEN, with_y=need_ctx)
    h_ctx = res_c[-1]
    yf_l, yb_l, _ = _ssd(xbc_l, bt_of(xbc_l, SEQ), o_dt, lw['ssm_a_log'], lw['ssm_dt_bias'], h_ctx,
                         nb=nb, t=SEQ, with_y=True)

    wb = lw['w_branch'].astype(BF16)
    wo = lw['w_out'].astype(BF16)
    w1, w3, w2 = lw['ffn_w1'].astype(BF16), lw['ffn_w3'].astype(BF16), lw['ffn_w2'].astype(BF16)
    x2 = _merge((y_na, y_mla, y_gqa), yf_l, yb_l, xbc_l, o_ssm, lw['ssm_d'], lw['ssm_g_norm'],
                gate, wb, wo, x2, gt1, lw['g_post1'], tpm)
    x2 = _ffn(x2, lw['g_pre2'], sc2, sh2, w1, w3, w2, gt2, lw['g_post2'], tpm)

    if need_ctx:
        yc_na = _attention(c_na, [(c_na, c_na, CTX_LEN, 1, 2)], nb=nb, t=CTX_LEN, dq=HALF,
                           tq=CTX_LEN, scale=NA_HEAD_DIM ** -0.5 * LOG2E)
        yc_mla = _attention(qc, [(kc, vc, CTX_LEN, 0, 0)], nb=nb, t=CTX_LEN, dq=LANES, tq=CTX_LEN)
        yc_gqa = _attention(gqc, [(gkc, gvc, CTX_LEN, 0, 0)], nb=nb, t=CTX_LEN, dq=HALF, tq=CTX_LEN)
        xc2 = _merge((yc_na, yc_mla, yc_gqa), res_c[0], res_c[1], xbc_c, c_ssm, lw['ssm_d'], lw['ssm_g_norm'],
                     c_gate, wb, wo, xc2, cgt1, lw['g_post1'], None)
        xc2 = _ffn(xc2, lw['g_pre2'], csc2, csh2, w1, w3, w2, cgt2, lw['g_post2'], None)
    return x2, xc2


def kernel(x, c, ctx, c_ctx, w_ada, b_ada, g_pre1, g_post1, g_pre2, g_post2, w_in, na_rpb, mla_g_q, mla_g_kv, mla_w_uq, mla_w_ukv, gqa_g_q, gqa_g_k, ssm_conv_w, ssm_conv_b, ssm_a_log, ssm_dt_bias, ssm_d, ssm_g_norm, w_branch, w_out, ffn_w1, ffn_w3, ffn_w2):
    nb, s, d = x.shape
    assert s == SEQ and d == D_MODEL and ctx.shape[1] == CTX_LEN
    depth = w_ada.shape[0]
    tables = {'mla': _rope_tables(MLA_ROPE, MLA_NOPE, LANES), 'gqa': _rope_tables(GQA_HEAD_DIM, 0, GQA_HEAD_DIM)}
    x2 = x.reshape(nb * s, d)
    xc2 = ctx.reshape(nb * CTX_LEN, d)
    n_rows = -(-(nb + 1) // 8) * 8
    c_rows = jnp.zeros((n_rows, d), F32).at[:nb].set(c).at[nb].set(c_ctx)
    stacked = dict(w_in=w_in, na_rpb=na_rpb, mla_g_q=mla_g_q, mla_g_kv=mla_g_kv, mla_w_uq=mla_w_uq,
                   mla_w_ukv=mla_w_ukv, gqa_g_q=gqa_g_q, gqa_g_k=gqa_g_k, ssm_conv_w=ssm_conv_w,
                   ssm_conv_b=ssm_conv_b, ssm_a_log=ssm_a_log, ssm_dt_bias=ssm_dt_bias, ssm_d=ssm_d,
                   ssm_g_norm=ssm_g_norm, w_branch=w_branch, w_out=w_out, ffn_w1=ffn_w1, ffn_w3=ffn_w3,
                   ffn_w2=ffn_w2, g_pre1=g_pre1, g_post1=g_post1, g_pre2=g_pre2, g_post2=g_post2)
    for k in ('w_in', 'mla_w_uq', 'mla_w_ukv', 'w_branch', 'w_out', 'ffn_w1', 'ffn_w3', 'ffn_w2'):
        stacked[k] = stacked[k].astype(BF16)
    for l in range(depth):
        lw = {k: v[l] for k, v in stacked.items()}
        mod_all = _ada(c_rows, w_ada[l], b_ada[l])
        x2, xc2 = _layer(x2, xc2, mod_all[:nb], mod_all[nb:nb + 1], lw, tables, nb, l < depth - 1)
    return x2.reshape(nb, s, d)
```

```python
import functools

import numpy as np
import jax
import jax.numpy as jnp
from jax import lax
from jax.experimental import pallas as pl
from jax.experimental.pallas import tpu as pltpu

F32 = jnp.float32
BF16 = jnp.bfloat16
HIGHEST = lax.Precision.HIGHEST

D_MODEL = 1024
SEQ = 2048
GRID_W = 64
GRID_ROWS = SEQ // GRID_W
CTX_LEN = 256
N_BRANCH = 4
BRANCH_W = 512
ROPE_THETA = 10000.0
EPS = 1e-6
NA_HEADS = 8
NA_HEAD_DIM = 64
NA_ROW_WIN = 8
NA_COL_WIN = 16
MLA_HEADS = 8
MLA_Q_RANK = 384
MLA_KV_RANK = 256
MLA_NOPE = 64
MLA_ROPE = 32
MLA_V = 64
GQA_Q_HEADS = 8
GQA_KV_HEADS = 2
GQA_HEAD_DIM = 64
SSM_HEADS = 8
SSM_HEAD_DIM = 64
SSM_GROUPS = 2
SSM_STATE = 128
SSM_CONV = 5
SSM_CHUNK = 128
SSM_INNER = SSM_HEADS * SSM_HEAD_DIM
SSM_BC = SSM_GROUPS * SSM_STATE
FFN_HIDDEN = ((8 * D_MODEL + 3 * 256 - 1) // (3 * 256)) * 256

KEY_SPLITS = (('na_k', 512), ('na_v', 512), ('mla_ckv', MLA_KV_RANK), ('mla_kr', MLA_ROPE),
              ('gqa_k', 128), ('gqa_v', 128), ('ssm_x', SSM_INNER), ('ssm_B', SSM_BC), ('ssm_dt', 2 * SSM_HEADS))
QUERY_SPLITS = (('na_q', 512), ('mla_cq', MLA_Q_RANK), ('gqa_q', 512), ('ssm_C', SSM_BC), ('ssm_z', SSM_INNER))
MIX_COLS = sum(w for _, w in KEY_SPLITS + QUERY_SPLITS)

LANES = 128
HALF = 64
N_PAIRS = 4
NEG = -1e30
LOG2E = 1.4426950408889634
NA_BLK = 4
NA_BLOCKS = GRID_ROWS // NA_BLK
NA_WIN_ROWS = 12
ATTN_HEADS = 8
QUAD = 4
QUAD_W = QUAD * HALF
VMEM_LIMIT = 58 * 1024 * 1024
IN_PROJ_TM = 512


def _params(n_axes):
    return pltpu.CompilerParams(dimension_semantics=("arbitrary",) * n_axes, vmem_limit_bytes=VMEM_LIMIT)


def _resident(shape):
    nd = len(shape)
    return pl.BlockSpec(shape, lambda *_: (0,) * nd, pipeline_mode=pl.Buffered(1))


def _dot(a, b):
    return jnp.dot(a, b, preferred_element_type=F32)


def _dot_nt(a, b):
    return lax.dot_general(a, b, (((1,), (1,)), ((), ())), preferred_element_type=F32)


def _silu(x):
    return x * jax.nn.sigmoid(x)


def _rms(x, g):
    return x * lax.rsqrt(jnp.mean(x * x, axis=-1, keepdims=True) + EPS) * g


def _lane(shape=(1, LANES)):
    return lax.broadcasted_iota(jnp.int32, shape, len(shape) - 1)


def _ada_kernel(c_ref, w_ref, b_ref, o_ref):
    a = _silu(c_ref[...]).astype(BF16)
    o_ref[...] = _dot(a, w_ref[...].astype(BF16)) + b_ref[...]


def _ada(c_rows, w_ada, b_ada):
    r, d = c_rows.shape
    n = w_ada.shape[1]
    tn = 1536
    return pl.pallas_call(
        _ada_kernel,
        grid=(n // tn,),
        in_specs=[pl.BlockSpec((r, d), lambda j: (0, 0)),
                  pl.BlockSpec((d, tn), lambda j: (0, j)),
                  pl.BlockSpec((1, tn), lambda j: (0, j))],
        out_specs=pl.BlockSpec((r, tn), lambda j: (0, j)),
        out_shape=jax.ShapeDtypeStruct((r, n), F32),
        compiler_params=_params(1),
        name="ada",
    )(c_rows, w_ada, b_ada.reshape(1, n))


def _in_proj_kernel(*refs, with_gate, rotate):
    refs = list(refs)
    take = lambda n: [refs.pop(0) for _ in range(n)]
    x_ref, g_ref, sc_ref, sh_ref = take(4)
    w_na, w_mla, w_gqa, w_ssm, w_dt = take(5)
    w_gate = take(1)[0] if with_gate else None
    mla_params = take(5)
    gqa_params = take(3)
    mla_tabs = take(2) if rotate else (None, None)
    gqa_tabs = take(2) if rotate else (None, None)
    o_na, q_mla, k_mla, v_mla, q_gqa, k_gqa, v_gqa, o_ssm, o_dt = take(9)
    hb = (_rms(x_ref[...], g_ref[...]) * (1.0 + sc_ref[0]) + sh_ref[0]).astype(BF16)

    def project(w_ref, o_ref):
        width = w_ref.shape[1]
        for c0 in range(0, width, 512):
            cw = min(512, width - c0)
            o_ref[:, c0:c0 + cw] = _dot(hb, w_ref[:, c0:c0 + cw]).astype(o_ref.dtype)

    project(w_na, o_na)
    _mla_prep(_dot(hb, w_mla[...]), *mla_params, *mla_tabs, q_mla, k_mla, v_mla)
    _gqa_prep(_dot(hb, w_gqa[...]), *gqa_params, *gqa_tabs, q_gqa, k_gqa, v_gqa)
    project(w_ssm, o_ssm)
    project(w_dt, o_dt)
    if with_gate:
        project(w_gate, refs.pop(0))


def _in_proj(x2, g, sc, sh, tiles_per_mod, w, lw, tables, with_gate, tm=IN_PROJ_TM):
    m, d = x2.shape
    rotate = tables is not None
    mod_map = (lambda i: (0, 0, 0)) if tiles_per_mod is None else (lambda i: (i // tiles_per_mod, 0, 0))
    weights = [w['na'], w['mla'], w['gqa'], w['ssm'], w['dt']] + ([w['gate']] if with_gate else [])
    bd = jnp.asarray(np.kron(np.eye(2), np.ones((HALF, HALF))), BF16)
    params = [lw['mla_g_q'].reshape(1, -1), lw['mla_g_kv'].reshape(1, -1), w['mla_uq'], w['mla_uk'], w['mla_uv'],
              jnp.tile(lw['gqa_g_q'], 2).reshape(1, LANES), jnp.tile(lw['gqa_g_k'], 2).reshape(1, LANES), bd]
    in_specs = [pl.BlockSpec((tm, d), lambda i: (i, 0)), _resident((1, d)),
                pl.BlockSpec((1, 1, d), mod_map), pl.BlockSpec((1, 1, d), mod_map)]
    in_specs += [_resident(a.shape) for a in weights + params]
    args = [x2, g.reshape(1, d), sc, sh] + weights + params
    if rotate:
        nt = SEQ // tm
        in_specs += [pl.BlockSpec((tm, LANES), lambda i: (i % nt, 0))] * 4
        args += list(tables['mla']) + list(tables['gqa'])
    outs = [(w['na'].shape[1], BF16),
            (MLA_HEADS * LANES, BF16), (MLA_HEADS * LANES, BF16), (MLA_HEADS * MLA_V, BF16),
            (GQA_Q_HEADS * GQA_HEAD_DIM, BF16), (GQA_KV_HEADS * QUAD_W, BF16), (GQA_KV_HEADS * QUAD_W, BF16),
            (w['ssm'].shape[1], BF16), (w['dt'].shape[1], F32)]
    if with_gate:
        outs.append((w['gate'].shape[1], BF16))
    return pl.pallas_call(
        functools.partial(_in_proj_kernel, with_gate=with_gate, rotate=rotate),
        grid=(m // tm,),
        in_specs=in_specs,
        out_specs=[pl.BlockSpec((tm, n), lambda i: (i, 0)) for n, _ in outs],
        out_shape=[jax.ShapeDtypeStruct((m, n), dt) for n, dt in outs],
        compiler_params=_params(1),
        name="in_proj",
    )(*args)


def _na_kernel(q_ref, k_ref, v_ref, kc_ref, vc_ref, bias_ref, o_ref):
    lo = _lane() < HALF
    kc = kc_ref[...]
    vc = vc_ref[...]
    qscale = NA_HEAD_DIM ** -0.5 * LOG2E
    nq = NA_BLK * GRID_W
    nk = NA_WIN_ROWS * GRID_W
    for bi in range(NA_BLOCKS):
        r0, ws = bi * NA_BLK, _na_window_start(bi)
        q = q_ref[r0 * GRID_W:r0 * GRID_W + nq, :]
        q = (q.astype(F32) * qscale).astype(q.dtype)
        kw = k_ref[ws * GRID_W:ws * GRID_W + nk, :]
        vw = v_ref[ws * GRID_W:ws * GRID_W + nk, :]
        zero = jnp.zeros_like(q)
        q2 = jnp.concatenate([jnp.where(lo, q, zero), jnp.where(lo, zero, q)], axis=0)
        s_loc = _dot_nt(q2, kw) + bias_ref[0, _na_block_kind(bi)]
        s_ctx = _dot_nt(q2, kc)
        m = jnp.maximum(jnp.max(s_loc, axis=-1, keepdims=True), jnp.max(s_ctx, axis=-1, keepdims=True))
        p_loc = jnp.exp2(s_loc - m)
        p_ctx = jnp.exp2(s_ctx - m)
        den = jnp.sum(p_loc, axis=-1, keepdims=True) + jnp.sum(p_ctx, axis=-1, keepdims=True)
        o = (_dot(p_loc.astype(BF16), vw) + _dot(p_ctx.astype(BF16), vc)) / den
        o_ref[r0 * GRID_W:r0 * GRID_W + nq, :] = jnp.where(lo, o[:nq], o[nq:]).astype(o_ref.dtype)


def _na_window_start(bi):
    return min(max(bi * NA_BLK - NA_ROW_WIN // 2, 0), GRID_ROWS - NA_WIN_ROWS)


def _na_block_kind(bi):
    return 0 if bi == 0 else (2 if bi == NA_BLOCKS - 1 else 1)


def _na_bias_table(rpb_all):
    n_heads = rpb_all.shape[0] * NA_HEADS
    n_rho = 2 * NA_ROW_WIN - 1
    n_off = 2 * NA_COL_WIN - 1
    w = np.arange(GRID_W)
    j = np.arange(GRID_W)
    cs = np.clip(w - NA_COL_WIN // 2, 0, GRID_W - NA_COL_WIN)
    valid = (j[None, :] >= cs[:, None]) & (j[None, :] < cs[:, None] + NA_COL_WIN)
    r = rpb_all.astype(F32).reshape(n_heads, n_rho, n_off) * LOG2E
    r = jnp.pad(r, ((0, 0), (0, 0), (0, GRID_W)))
    period = n_off + GRID_W
    skew = jnp.broadcast_to(r[:, :, None, :], (n_heads, n_rho, GRID_W, period)).reshape(n_heads, n_rho, GRID_W * period)
    skew = skew[:, :, :GRID_W * (period - 1)].reshape(n_heads, n_rho, GRID_W, period - 1)
    t = jnp.where(valid[None, None], skew[:, :, :, NA_COL_WIN - 1:NA_COL_WIN - 1 + GRID_W], NEG)
    tt = t.transpose(0, 2, 1, 3).reshape(n_heads, GRID_W, n_rho * GRID_W)
    ttp = jnp.pad(tt, ((0, 0), (0, 0), (NA_BLK * GRID_W, NA_BLK * GRID_W)), constant_values=NEG)
    nk = NA_WIN_ROWS * GRID_W
    patterns = {}
    for bi in range(NA_BLOCKS):
        ws = _na_window_start(bi)
        pat = []
        for u in range(NA_BLK):
            r = bi * NA_BLK + u
            rs = min(max(r - NA_ROW_WIN // 2, 0), GRID_ROWS - NA_ROW_WIN)
            rho0 = ws - r + NA_ROW_WIN - 1
            in_win = tuple(0 <= ws + i - rs < NA_ROW_WIN for i in range(NA_WIN_ROWS))
            pat.append((rho0, in_win))
        assert patterns.setdefault(_na_block_kind(bi), pat) == pat
    kinds = []
    for kind in range(3):
        per_u = []
        for rho0, in_win in patterns[kind]:
            start = (rho0 + NA_BLK) * GRID_W
            mask = np.repeat(np.asarray(in_win), GRID_W)
            per_u.append(jnp.where(mask[None, None, :], ttp[:, :, start:start + nk], NEG))
        kinds.append(jnp.stack(per_u, axis=1))
    b = jnp.stack(kinds, axis=1)
    b = b.reshape(n_heads // 2, 2, 3, NA_BLK * GRID_W, nk).transpose(0, 2, 1, 3, 4)
    return b.reshape(n_heads // 2, 3, 2 * NA_BLK * GRID_W, nk)


def _na_attention(o_na_l, o_na_c, bias, layer, nb):
    return pl.pallas_call(
        _na_kernel,
        grid=(nb, N_PAIRS),
        in_specs=[pl.BlockSpec((SEQ, LANES), lambda b, p: (b, p)),
                  pl.BlockSpec((SEQ, LANES), lambda b, p: (b, N_PAIRS + p)),
                  pl.BlockSpec((SEQ, LANES), lambda b, p: (b, 2 * N_PAIRS + p)),
                  pl.BlockSpec((CTX_LEN, LANES), lambda b, p: (b, N_PAIRS + p)),
                  pl.BlockSpec((CTX_LEN, LANES), lambda b, p: (b, 2 * N_PAIRS + p)),
                  pl.BlockSpec((1, 3, 2 * NA_BLK * GRID_W, NA_WIN_ROWS * GRID_W), lambda b, p: (layer * N_PAIRS + p, 0, 0, 0))],
        out_specs=pl.BlockSpec((SEQ, LANES), lambda b, p: (b, p)),
        out_shape=jax.ShapeDtypeStruct((nb * SEQ, NA_HEADS * NA_HEAD_DIM), BF16),
        compiler_params=_params(2),
        name="na_attn",
    )(o_na_l, o_na_l, o_na_l, o_na_c, o_na_c, bias)


def _attn_kernel(*refs, n_src, dq, scale):
    q_ref = refs[0]
    o_ref = refs[-1]
    lo = _lane() < HALF
    lane_o = _lane((1, QUAD_W))
    for quad in range(ATTN_HEADS // QUAD):
        vsl = slice(quad * QUAD_W, (quad + 1) * QUAD_W)
        outs = []
        for a in range(quad * QUAD, (quad + 1) * QUAD):
            c0 = a * LANES if dq == LANES else (a // 2) * LANES
            qa = q_ref[:, c0:c0 + LANES]
            if scale != 1.0:
                qa = (qa.astype(F32) * scale).astype(qa.dtype)
            if dq != LANES:
                zero = jnp.zeros_like(qa)
                qa = jnp.where(lo, qa, zero) if a % 2 == 0 else jnp.where(lo, zero, qa)
            ss = [_dot_nt(qa, refs[1 + 2 * s][:, c0:c0 + LANES]) for s in range(n_src)]
            m = ss[0].max(axis=-1, keepdims=True)
            for sc in ss[1:]:
                m = jnp.maximum(m, sc.max(axis=-1, keepdims=True))
            den = None
            o = None
            for s in range(n_src):
                p = jnp.exp2(ss[s] - m)
                ps = jnp.sum(p, axis=-1, keepdims=True)
                po = _dot(p.astype(BF16), refs[2 + 2 * s][:, vsl])
                den = ps if den is None else den + ps
                o = po if o is None else o + po
            outs.append(o / den)
        out = outs[QUAD - 1]
        for a in range(QUAD - 2, -1, -1):
            out = jnp.where(lane_o < (a + 1) * HALF, outs[a], out)
        o_ref[:, vsl] = out.astype(o_ref.dtype)


def _attention(q, srcs, *, nb, t, dq, tq, scale=1.0):
    nq = t // tq
    in_specs = [pl.BlockSpec((tq, ATTN_HEADS * dq), lambda b, i: (b * nq + i, 0))]
    args = [q]
    for k_arr, v_arr, u, k_off, v_off in srcs:
        in_specs.append(pl.BlockSpec((u, ATTN_HEADS * dq), lambda b, i, k_off=k_off: (b, k_off)))
        in_specs.append(pl.BlockSpec((u, ATTN_HEADS * HALF), lambda b, i, v_off=v_off: (b, v_off)))
        args += [k_arr, v_arr]
    return pl.pallas_call(
        functools.partial(_attn_kernel, n_src=len(srcs), dq=dq, scale=scale),
        grid=(nb, nq),
        in_specs=in_specs,
        out_specs=pl.BlockSpec((tq, ATTN_HEADS * HALF), lambda b, i: (b * nq + i, 0)),
        out_shape=jax.ShapeDtypeStruct((nb * t, ATTN_HEADS * HALF), BF16),
        compiler_params=_params(2),
        name="attn",
    )(*args)


def _mla_prep(x, gq_ref, gkv_ref, wq_ref, wk_ref, wv_ref, cos_ref, sin_ref, q_ref, k_ref, v_ref):
    rotate = cos_ref is not None
    scale = (MLA_NOPE + MLA_ROPE) ** -0.5 * LOG2E
    cq = x[:, :MLA_Q_RANK]
    ckv = x[:, MLA_Q_RANK:MLA_Q_RANK + MLA_KV_RANK]
    kr = x[:, MLA_Q_RANK + MLA_KV_RANK:]
    cqn = _rms(cq, gq_ref[...]).astype(BF16)
    ckvn = _rms(ckv, gkv_ref[...]).astype(BF16)
    lane = _lane()
    first_half = lane < MLA_NOPE + MLA_ROPE // 2

    def rope(t):
        if not rotate:
            return t
        rot = jnp.where(first_half, pltpu.roll(t, LANES - MLA_ROPE // 2, axis=1), pltpu.roll(t, MLA_ROPE // 2, axis=1))
        return t * cos_ref[...] + rot * sin_ref[...]

    kr = rope(kr)
    for h in range(MLA_HEADS):
        sl = slice(h * LANES, (h + 1) * LANES)
        qh = rope(_dot(cqn, wq_ref[:, sl]))
        q_ref[:, sl] = (qh * scale).astype(q_ref.dtype)
        k_ref[:, sl] = (_dot(ckvn, wk_ref[:, sl]) + kr).astype(k_ref.dtype)
    v_ref[...] = _dot(ckvn, wv_ref[...]).astype(v_ref.dtype)


def _gqa_prep(x, gq_ref, gk_ref, bd_ref, cos_ref, sin_ref, q_ref, k_ref, v_ref):
    rotate = cos_ref is not None
    scale = GQA_HEAD_DIM ** -0.5 * LOG2E
    lane = _lane()
    lo = lane < HALF
    first_half = (lane & (GQA_HEAD_DIM - 1)) < GQA_HEAD_DIM // 2

    def headnorm(t, g):
        ss = _dot((t * t).astype(BF16), bd_ref[...]) * (1.0 / GQA_HEAD_DIM)
        return t * lax.rsqrt(ss + EPS) * g

    def rope(t):
        if not rotate:
            return t
        rot = jnp.where(first_half, pltpu.roll(t, LANES - GQA_HEAD_DIM // 2, axis=1), pltpu.roll(t, GQA_HEAD_DIM // 2, axis=1))
        return t * cos_ref[...] + rot * sin_ref[...]

    for j in range(N_PAIRS):
        sl = slice(j * LANES, (j + 1) * LANES)
        q_ref[:, sl] = (rope(headnorm(x[:, sl], gq_ref[...])) * scale).astype(q_ref.dtype)
    nq = GQA_Q_HEADS * GQA_HEAD_DIM
    k = rope(headnorm(x[:, nq:nq + LANES], gk_ref[...]))
    v = x[:, nq + LANES:nq + 2 * LANES]
    for t, ref in ((k, k_ref), (v, v_ref)):
        sw = pltpu.roll(t, HALF, axis=1)
        for g, dup in enumerate((jnp.where(lo, t, sw), jnp.where(lo, sw, t))):
            dup = dup.astype(ref.dtype)
            ref[:, (2 * g) * LANES:(2 * g + 1) * LANES] = dup
            ref[:, (2 * g + 1) * LANES:(2 * g + 2) * LANES] = dup


def _rope_tables(dim, lane_start, group):
    t = jnp.arange(SEQ)
    row = (t // GRID_W).astype(F32)
    col = (t % GRID_W).astype(F32)
    quarter = dim // 4
    inv = ROPE_THETA ** (-jnp.arange(quarter, dtype=F32) / quarter)
    ang = jnp.concatenate([row[:, None] * inv, col[:, None] * inv], axis=-1)
    cos, sin = jnp.cos(ang), jnp.sin(ang)
    cos_g = jnp.ones((SEQ, group), F32).at[:, lane_start:lane_start + dim].set(jnp.concatenate([cos, cos], axis=-1))
    sin_g = jnp.zeros((SEQ, group), F32).at[:, lane_start:lane_start + dim].set(jnp.concatenate([-sin, sin], axis=-1))
    reps = LANES // group
    return jnp.tile(cos_g, (1, reps)), jnp.tile(sin_g, (1, reps))


CONV_HALO = 16


def _conv_kernel(x_ref, prev_ref, next_ref, w_ref, b_ref, o_ref):
    i = pl.program_id(1)
    n = pl.num_programs(1)
    tm = x_ref.shape[0]
    rows = tm + 2 * CONV_HALO
    for c0 in range(0, x_ref.shape[1], LANES):
        sl = slice(c0, c0 + LANES)
        prev = jnp.where(i > 0, prev_ref[:, sl].astype(F32), 0.0)
        nxt = jnp.where(i < n - 1, next_ref[:, sl].astype(F32), 0.0)
        xcat = jnp.concatenate([prev, x_ref[:, sl].astype(F32), nxt], axis=0)
        acc = jnp.zeros((tm, LANES), F32) + b_ref[:, sl]
        for k in range(SSM_CONV):
            sh = (SSM_CONV // 2 - k) % rows
            rolled = xcat if sh == 0 else pltpu.roll(xcat, sh, axis=0)
            acc = acc + rolled[CONV_HALO:CONV_HALO + tm] * w_ref[k:k + 1, sl]
        o_ref[:, sl] = _silu(acc).astype(o_ref.dtype)


def _ssm_conv(o_ssm, conv_w, conv_b, nb, t, tm=256):
    nch = conv_w.shape[1]
    nt = t // tm
    hb = tm // CONV_HALO
    total = nb * t // CONV_HALO
    w8 = jnp.zeros((8, nch), F32).at[:SSM_CONV].set(conv_w)
    return pl.pallas_call(
        _conv_kernel,
        grid=(nb, nt),
        in_specs=[pl.BlockSpec((tm, nch), lambda b, i: (b * nt + i, 0)),
                  pl.BlockSpec((CONV_HALO, nch), lambda b, i: (jnp.maximum((b * nt + i) * hb - 1, 0), 0)),
                  pl.BlockSpec((CONV_HALO, nch), lambda b, i: (jnp.minimum((b * nt + i + 1) * hb, total - 1), 0)),
                  _resident((8, nch)), _resident((1, nch))],
        out_specs=pl.BlockSpec((tm, nch), lambda b, i: (b * nt + i, 0)),
        out_shape=jax.ShapeDtypeStruct((nb * t, nch), BF16),
        compiler_params=_params(2),
        name="ssm_conv",
    )(o_ssm, o_ssm, o_ssm, w8, conv_b.reshape(1, nch))


def _softplus(x):
    return jnp.maximum(x, 0.0) + jnp.log(1.0 + jnp.exp(-jnp.abs(x)))


def _ssd_kernel(*refs, with_y):
    n_dir = 4 if with_y else 3
    dir_refs = (refs[:n_dir], refs[n_dir:2 * n_dir])
    al_ref, bi_ref, h0_ref = refs[2 * n_dir:2 * n_dir + 3]
    if with_y:
        yf_ref, yb_ref, hf_ref, st = refs[2 * n_dir + 3:]
        y_refs = (yf_ref, yb_ref)
    else:
        hf_ref, st = refs[2 * n_dir + 3:]
    j = pl.program_id(1)
    q = SSM_CHUNK

    @pl.when(j == 0)
    def _():
        st[...] = h0_ref[:, 0]

    row = lax.broadcasted_iota(jnp.int32, (q, q), 0)
    col = lax.broadcasted_iota(jnp.int32, (q, q), 1)
    lane = _lane()
    lo = lane < HALF

    def col_of(x, h):
        return jnp.sum(jnp.where(lane == h, x, 0.0), axis=1, keepdims=True)

    for d in range(2):
        if with_y:
            x_ref, b_ref, c_ref, dt_ref = dir_refs[d]
        else:
            x_ref, b_ref, dt_ref = dir_refs[d]
        m_qs = (col <= row) if d == 0 else (col >= row)
        dt = _softplus(dt_ref[...] + bi_ref[d])
        da = jnp.where(lane < SSM_HEADS, dt * (-jnp.exp(al_ref[d])), 0.0)
        a_cum_c = jnp.dot(jnp.where(m_qs, 1.0, 0.0), da, precision=HIGHEST, preferred_element_type=F32)
        a_tot = jnp.sum(da, axis=0, keepdims=True)
        a_cum_r = a_cum_c.T

        x = x_ref[...].astype(F32)
        if with_y:
            cm = c_ref[...]
        for g in range(SSM_GROUPS):
            btg = b_ref[:, g * SSM_STATE:(g + 1) * SSM_STATE].astype(F32).T.astype(BF16)
            if with_y:
                cg = cm[:, g * SSM_STATE:(g + 1) * SSM_STATE]
                cb = _dot(cg, btg)
            for pp in range(2):
                p = 2 * g + pp
                h0, h1 = 2 * p, 2 * p + 1
                ac0, ac1 = col_of(a_cum_c, h0), col_of(a_cum_c, h1)
                tot0, tot1 = col_of(a_tot, h0), col_of(a_tot, h1)
                xdt = x[:, p * LANES:(p + 1) * LANES] * jnp.where(lo, col_of(dt, h0), col_of(dt, h1))
                ht = st[d, p]
                if with_y:
                    xdt_b = xdt.astype(BF16)
                    l0 = jnp.exp(jnp.where(m_qs, ac0 - a_cum_r[h0:h0 + 1, :], NEG))
                    l1 = jnp.exp(jnp.where(m_qs, ac1 - a_cum_r[h1:h1 + 1, :], NEG))
                    y_in = jnp.where(lo, _dot((cb * l0).astype(BF16), xdt_b), _dot((cb * l1).astype(BF16), xdt_b))
                    y_st = _dot(cg, ht.astype(BF16)) * jnp.where(lo, jnp.exp(ac0), jnp.exp(ac1))
                    y_refs[d][:, p * LANES:(p + 1) * LANES] = (y_in + y_st).astype(y_refs[d].dtype)
                dec = jnp.where(lo, jnp.exp(tot0 - ac0), jnp.exp(tot1 - ac1))
                st[d, p] = ht * jnp.where(lo, jnp.exp(tot0), jnp.exp(tot1)) + _dot(btg, (xdt * dec).astype(BF16))

    @pl.when(j == pl.num_programs(1) - 1)
    def _():
        hf_ref[:, 0] = st[...]


def _ssd(xbc, o_dt, a_log, dt_bias, h0, *, nb, t, with_y):
    nc = t // SSM_CHUNK
    pad = LANES - SSM_HEADS
    al_row = jnp.pad(a_log.astype(F32), ((0, 0), (0, pad))).reshape(2, 1, LANES)
    bi_row = jnp.pad(dt_bias.astype(F32), ((0, 0), (0, pad))).reshape(2, 1, LANES)
    x_blk = SSM_INNER // SSM_BC
    in_specs, args = [], []
    for d in range(2):
        cidx = (lambda j: j) if d == 0 else (lambda j: nc - 1 - j)
        in_specs += [pl.BlockSpec((SSM_CHUNK, SSM_INNER), lambda b, j, c=cidx: (b * nc + c(j), 0)),
                     pl.BlockSpec((SSM_CHUNK, SSM_BC), lambda b, j, c=cidx: (b * nc + c(j), x_blk))]
        args += [xbc, xbc]
        if with_y:
            in_specs.append(pl.BlockSpec((SSM_CHUNK, SSM_BC), lambda b, j, c=cidx: (b * nc + c(j), x_blk + 1)))
            args.append(xbc)
        in_specs.append(pl.BlockSpec((SSM_CHUNK, LANES), lambda b, j, c=cidx, d=d: (b * nc + c(j), d)))
        args.append(o_dt)
    st_spec = pl.BlockSpec((2, 1, N_PAIRS, SSM_STATE, LANES), lambda b, j: (0, b, 0, 0, 0))
    in_specs += [_resident(al_row.shape), _resident(bi_row.shape), st_spec]
    args += [al_row, bi_row, h0]
    st_shape = jax.ShapeDtypeStruct((2, nb, N_PAIRS, SSM_STATE, LANES), F32)
    out_specs, out_shape = [st_spec], [st_shape]
    if with_y:
        out_specs = [pl.BlockSpec((SSM_CHUNK, SSM_INNER), lambda b, j: (b * nc + j, 0)),
                     pl.BlockSpec((SSM_CHUNK, SSM_INNER), lambda b, j: (b * nc + nc - 1 - j, 0))] + out_specs
        out_shape = [jax.ShapeDtypeStruct((nb * t, SSM_INNER), BF16)] * 2 + out_shape
    return pl.pallas_call(
        functools.partial(_ssd_kernel, with_y=with_y),
        grid=(nb, nc),
        in_specs=in_specs, out_specs=out_specs, out_shape=out_shape,
        scratch_shapes=[pltpu.VMEM((2, N_PAIRS, SSM_STATE, LANES), F32)],
        compiler_params=_params(2),
        name="ssd",
    )(*args)


def _merge_kernel(yna_ref, ymla_ref, ygqa_ref, yf_ref, yb_ref, xs_ref, z_ref, dsk_ref, gssm_ref,
                  gate_ref, wb_ref, wo_ref, x_ref, gt_ref, g_ref, o_ref):
    y = yf_ref[...].astype(F32) + yb_ref[...].astype(F32) + dsk_ref[...] * xs_ref[...].astype(F32)
    y_ssm = _rms(y * _silu(z_ref[...].astype(F32)), gssm_ref[...]).astype(BF16)
    acc = None
    for k, yk in enumerate((yna_ref[...], ymla_ref[...], ygqa_ref[...], y_ssm)):
        proj = _dot(yk, wb_ref[k])
        gate = jax.nn.sigmoid(gate_ref[:, k * D_MODEL:(k + 1) * D_MODEL].astype(F32))
        acc = gate * proj if acc is None else acc + gate * proj
    mix = _dot(acc.astype(BF16), wo_ref[...])
    o_ref[...] = x_ref[...] + gt_ref[0] * _rms(mix, g_ref[...])


def _merge(ys, y_f, y_b, xbc, o_ssm, d_skip, g_norm, gate, wb, wo, x2, gt, g_post, tiles_per_mod, tm=512):
    m, d = x2.shape
    mod_map = (lambda i: (0, 0, 0)) if tiles_per_mod is None else (lambda i: (i // tiles_per_mod, 0, 0))
    dvec = jnp.repeat(d_skip.astype(F32), SSM_HEAD_DIM).reshape(1, SSM_INNER)
    z_blk = o_ssm.shape[1] // SSM_INNER - 1
    row_blk = pl.BlockSpec((tm, BRANCH_W), lambda i: (i, 0))
    in_specs = [row_blk] * (N_BRANCH - 1) + [row_blk, row_blk, row_blk,
                                             pl.BlockSpec((tm, SSM_INNER), lambda i: (i, z_blk)),
                                             _resident((1, SSM_INNER)), _resident((1, SSM_INNER))]
    in_specs += [pl.BlockSpec((tm, N_BRANCH * d), lambda i: (i, 0)),
                 _resident(wb.shape), _resident(wo.shape),
                 pl.BlockSpec((tm, d), lambda i: (i, 0)),
                 pl.BlockSpec((1, 1, d), mod_map), _resident((1, d))]
    return pl.pallas_call(
        _merge_kernel,
        grid=(m // tm,),
        in_specs=in_specs,
        out_specs=pl.BlockSpec((tm, d), lambda i: (i, 0)),
        out_shape=jax.ShapeDtypeStruct((m, d), F32),
        compiler_params=_params(1),
        name="merge",
    )(*ys, y_f, y_b, xbc, o_ssm, dvec, g_norm.reshape(1, SSM_INNER), gate, wb, wo, x2, gt, g_post.reshape(1, d))


def _ffn_kernel(x_ref, gpre_ref, sc_ref, sh_ref, w1_ref, w3_ref, w2_ref, gt_ref, gpost_ref, o_ref):
    x = x_ref[...]
    hb = (_rms(x, gpre_ref[...]) * (1.0 + sc_ref[0]) + sh_ref[0]).astype(BF16)
    u = _silu(_dot(hb, w1_ref[...])) * _dot(hb, w3_ref[...])
    f = _dot(u.astype(BF16), w2_ref[...])
    o_ref[...] = x + gt_ref[0] * _rms(f, gpost_ref[...])


def _ffn(x2, g_pre, sc, sh, w1, w3, w2, gt, g_post, tiles_per_mod, tm=512):
    m, d = x2.shape
    mod_map = (lambda i: (0, 0, 0)) if tiles_per_mod is None else (lambda i: (i // tiles_per_mod, 0, 0))
    mod = pl.BlockSpec((1, 1, d), mod_map)
    return pl.pallas_call(
        _ffn_kernel,
        grid=(m // tm,),
        in_specs=[pl.BlockSpec((tm, d), lambda i: (i, 0)), _resident((1, d)), mod, mod,
                  _resident(w1.shape), _resident(w3.shape), _resident(w2.shape), mod, _resident((1, d))],
        out_specs=pl.BlockSpec((tm, d), lambda i: (i, 0)),
        out_shape=jax.ShapeDtypeStruct((m, d), F32),
        compiler_params=_params(1),
        name="ffn",
    )(x2, g_pre.reshape(1, d), sc, sh, w1, w3, w2, gt, g_post.reshape(1, d))


def _split_w_in(w_in):
    off, cols = 0, {}
    for name, w in KEY_SPLITS + QUERY_SPLITS:
        cols[name] = w_in[:, off:off + w]
        off += w
    d = w_in.shape[0]
    z = lambda n: jnp.zeros((d, n), w_in.dtype)
    cat = lambda parts: jnp.concatenate(parts, axis=1).astype(BF16)
    w_na = cat([cols['na_q'], cols['na_k'], cols['na_v']])
    w_mla = cat([cols['mla_cq'], cols['mla_ckv'], z(MLA_NOPE), cols['mla_kr'], z(LANES - MLA_NOPE - MLA_ROPE)])
    w_gqa = cat([cols['gqa_q'], cols['gqa_k'], cols['gqa_v']])
    w_ssm = cat([cols['ssm_x'], cols['ssm_B'], cols['ssm_C'], cols['ssm_z']])
    w_dt = cat([cols['ssm_dt'][:, :SSM_HEADS], z(LANES - SSM_HEADS), cols['ssm_dt'][:, SSM_HEADS:], z(LANES - SSM_HEADS)])
    w_gate = w_in[:, MIX_COLS:].astype(BF16)
    return dict(na=w_na, mla=w_mla, gqa=w_gqa, ssm=w_ssm, dt=w_dt, gate=w_gate)


def _layer(x2, xc2, mods, mc, lw, tables, layer, nb, need_ctx):
    d = D_MODEL
    tpm = SEQ // 512
    chunk = lambda a, k: a[:, k * d:(k + 1) * d].reshape(-1, 1, d)
    sh1, sc1, gt1, sh2, sc2, gt2 = (chunk(mods, k) for k in range(6))
    csh1, csc1, cgt1, csh2, csc2, cgt2 = (chunk(mc, k) for k in range(6))
    w = _split_w_in(lw['w_in'])
    w['mla_uq'] = jnp.pad(lw['mla_w_uq'].reshape(MLA_Q_RANK, MLA_HEADS, MLA_NOPE + MLA_ROPE),
                          ((0, 0), (0, 0), (0, LANES - MLA_NOPE - MLA_ROPE))).reshape(MLA_Q_RANK, MLA_HEADS * LANES)
    wkv = lw['mla_w_ukv'].reshape(MLA_KV_RANK, MLA_HEADS, MLA_NOPE + MLA_V)
    w['mla_uk'] = jnp.pad(wkv[:, :, :MLA_NOPE], ((0, 0), (0, 0), (0, LANES - MLA_NOPE))).reshape(MLA_KV_RANK, MLA_HEADS * LANES)
    w['mla_uv'] = wkv[:, :, MLA_NOPE:].reshape(MLA_KV_RANK, MLA_HEADS * MLA_V)
    tpm_in = SEQ // IN_PROJ_TM
    o_na, ql, kl, vl, gql, gkl, gvl, o_ssm, o_dt, gate = _in_proj(
        x2, lw['g_pre1'], sc1, sh1, tpm_in, w, lw, tables, True)
    res = _in_proj(xc2, lw['g_pre1'], csc1, csh1, None, w, lw, None, need_ctx)
    c_na, qc, kc, vc, gqc, gkc, gvc, c_ssm, c_dt = res[:9]
    c_gate = res[9] if need_ctx else None

    y_na = _na_attention(o_na, c_na, tables['na_bias'], layer, nb)
    y_mla = _attention(ql, [(kc, vc, CTX_LEN, 0, 0), (kl, vl, SEQ, 0, 0)], nb=nb, t=SEQ, dq=LANES, tq=256)
    y_gqa = _attention(gql, [(gkc, gvc, CTX_LEN, 0, 0), (gkl, gvl, SEQ, 0, 0)], nb=nb, t=SEQ, dq=HALF, tq=256)

    conv_w, conv_b = lw['ssm_conv_w'], lw['ssm_conv_b']
    xbc_l = _ssm_conv(o_ssm, conv_w, conv_b, nb, SEQ)
    xbc_c = _ssm_conv(c_ssm, conv_w, conv_b, nb, CTX_LEN)
    h_zero = jnp.zeros((2, nb, N_PAIRS, SSM_STATE, LANES), F32)
    res_c = _ssd(xbc_c, c_dt, lw['ssm_a_log'], lw['ssm_dt_bias'], h_zero, nb=nb, t=CTX_LEN, with_y=need_ctx)
    h_ctx = res_c[-1]
    yf_l, yb_l, _ = _ssd(xbc_l, o_dt, lw['ssm_a_log'], lw['ssm_dt_bias'], h_ctx, nb=nb, t=SEQ, with_y=True)

    wb = lw['w_branch'].astype(BF16)
    wo = lw['w_out'].astype(BF16)
    w1, w3, w2 = lw['ffn_w1'].astype(BF16), lw['ffn_w3'].astype(BF16), lw['ffn_w2'].astype(BF16)
    x2 = _merge((y_na, y_mla, y_gqa), yf_l, yb_l, xbc_l, o_ssm, lw['ssm_d'], lw['ssm_g_norm'],
                gate, wb, wo, x2, gt1, lw['g_post1'], tpm)
    x2 = _ffn(x2, lw['g_pre2'], sc2, sh2, w1, w3, w2, gt2, lw['g_post2'], tpm)

    if need_ctx:
        yc_na = _attention(c_na, [(c_na, c_na, CTX_LEN, 1, 2)], nb=nb, t=CTX_LEN, dq=HALF,
                           tq=CTX_LEN, scale=NA_HEAD_DIM ** -0.5 * LOG2E)
        yc_mla = _attention(qc, [(kc, vc, CTX_LEN, 0, 0)], nb=nb, t=CTX_LEN, dq=LANES, tq=CTX_LEN)
        yc_gqa = _attention(gqc, [(gkc, gvc, CTX_LEN, 0, 0)], nb=nb, t=CTX_LEN, dq=HALF, tq=CTX_LEN)
        xc2 = _merge((yc_na, yc_mla, yc_gqa), res_c[0], res_c[1], xbc_c, c_ssm, lw['ssm_d'], lw['ssm_g_norm'],
                     c_gate, wb, wo, xc2, cgt1, lw['g_post1'], None)
        xc2 = _ffn(xc2, lw['g_pre2'], csc2, csh2, w1, w3, w2, cgt2, lw['g_post2'], None)
    return x2, xc2


def kernel(x, c, ctx, c_ctx, w_ada, b_ada, g_pre1, g_post1, g_pre2, g_post2, w_in, na_rpb, mla_g_q, mla_g_kv, mla_w_uq, mla_w_ukv, gqa_g_q, gqa_g_k, ssm_conv_w, ssm_conv_b, ssm_a_log, ssm_dt_bias, ssm_d, ssm_g_norm, w_branch, w_out, ffn_w1, ffn_w3, ffn_w2):
    nb, s, d = x.shape
    assert s == SEQ and d == D_MODEL and ctx.shape[1] == CTX_LEN
    depth = w_ada.shape[0]
    tables = {'mla': _rope_tables(MLA_ROPE, MLA_NOPE, LANES), 'gqa': _rope_tables(GQA_HEAD_DIM, 0, GQA_HEAD_DIM),
              'na_bias': _na_bias_table(na_rpb)}
    x2 = x.reshape(nb * s, d)
    xc2 = ctx.reshape(nb * CTX_LEN, d)
    n_rows = -(-(nb + 1) // 8) * 8
    c_rows = jnp.zeros((n_rows, d), F32).at[:nb].set(c).at[nb].set(c_ctx)
    stacked = dict(w_in=w_in, na_rpb=na_rpb, mla_g_q=mla_g_q, mla_g_kv=mla_g_kv, mla_w_uq=mla_w_uq,
                   mla_w_ukv=mla_w_ukv, gqa_g_q=gqa_g_q, gqa_g_k=gqa_g_k, ssm_conv_w=ssm_conv_w,
                   ssm_conv_b=ssm_conv_b, ssm_a_log=ssm_a_log, ssm_dt_bias=ssm_dt_bias, ssm_d=ssm_d,
                   ssm_g_norm=ssm_g_norm, w_branch=w_branch, w_out=w_out, ffn_w1=ffn_w1, ffn_w3=ffn_w3,
                   ffn_w2=ffn_w2, g_pre1=g_pre1, g_post1=g_post1, g_pre2=g_pre2, g_post2=g_post2)
    for k in ('w_in', 'mla_w_uq', 'mla_w_ukv', 'w_branch', 'w_out', 'ffn_w1', 'ffn_w3', 'ffn_w2'):
        stacked[k] = stacked[k].astype(BF16)
    for l in range(depth):
        lw = {k: v[l] for k, v in stacked.items()}
        mod_all = _ada(c_rows, w_ada[l], b_ada[l])
        x2, xc2 = _layer(x2, xc2, mod_all[:nb], mod_all[nb:nb + 1], lw, tables, l, nb, l < depth - 1)
    return x2.reshape(nb, s, d)
```

```python
import functools

import numpy as np
import jax
import jax.numpy as jnp
from jax import lax
from jax.experimental import pallas as pl
from jax.experimental.pallas import tpu as pltpu

F32 = jnp.float32
BF16 = jnp.bfloat16
HIGHEST = lax.Precision.HIGHEST

D_MODEL = 1024
SEQ = 2048
GRID_W = 64
GRID_ROWS = SEQ // GRID_W
CTX_LEN = 256
N_BRANCH = 4
BRANCH_W = 512
ROPE_THETA = 10000.0
EPS = 1e-6
NA_HEADS = 8
NA_HEAD_DIM = 64
NA_ROW_WIN = 8
NA_COL_WIN = 16
MLA_HEADS = 8
MLA_Q_RANK = 384
MLA_KV_RANK = 256
MLA_NOPE = 64
MLA_ROPE = 32
MLA_V = 64
GQA_Q_HEADS = 8
GQA_KV_HEADS = 2
GQA_HEAD_DIM = 64
SSM_HEADS = 8
SSM_HEAD_DIM = 64
SSM_GROUPS = 2
SSM_STATE = 128
SSM_CONV = 5
SSM_CHUNK = 128
SSM_INNER = SSM_HEADS * SSM_HEAD_DIM
SSM_BC = SSM_GROUPS * SSM_STATE
FFN_HIDDEN = ((8 * D_MODEL + 3 * 256 - 1) // (3 * 256)) * 256

KEY_SPLITS = (('na_k', 512), ('na_v', 512), ('mla_ckv', MLA_KV_RANK), ('mla_kr', MLA_ROPE),
              ('gqa_k', 128), ('gqa_v', 128), ('ssm_x', SSM_INNER), ('ssm_B', SSM_BC), ('ssm_dt', 2 * SSM_HEADS))
QUERY_SPLITS = (('na_q', 512), ('mla_cq', MLA_Q_RANK), ('gqa_q', 512), ('ssm_C', SSM_BC), ('ssm_z', SSM_INNER))
MIX_COLS = sum(w for _, w in KEY_SPLITS + QUERY_SPLITS)

LANES = 128
HALF = 64
N_PAIRS = 4
NEG = -1e30
LOG2E = 1.4426950408889634
NA_BLK = 4
NA_BLOCKS = GRID_ROWS // NA_BLK
NA_WIN_ROWS = 12
ATTN_HEADS = 8
QUAD = 4
QUAD_W = QUAD * HALF
VMEM_LIMIT = 58 * 1024 * 1024
IN_PROJ_TM = 512
CONV_HALO = 16


def _params(n_axes):
    return pltpu.CompilerParams(dimension_semantics=("arbitrary",) * n_axes, vmem_limit_bytes=VMEM_LIMIT)


def _resident(shape):
    nd = len(shape)
    return pl.BlockSpec(shape, lambda *_: (0,) * nd, pipeline_mode=pl.Buffered(1))


def _dot(a, b):
    return jnp.dot(a, b, preferred_element_type=F32)


def _dot_nt(a, b):
    return lax.dot_general(a, b, (((1,), (1,)), ((), ())), preferred_element_type=F32)


def _silu(x):
    return x * jax.nn.sigmoid(x)


def _rms(x, g):
    return x * lax.rsqrt(jnp.mean(x * x, axis=-1, keepdims=True) + EPS) * g


def _lane(shape=(1, LANES)):
    return lax.broadcasted_iota(jnp.int32, shape, len(shape) - 1)


def _ada_kernel(c_ref, w_ref, b_ref, o_ref):
    a = _silu(c_ref[...]).astype(BF16)
    o_ref[...] = _dot(a, w_ref[...].astype(BF16)) + b_ref[...]


def _ada(c_rows, w_ada, b_ada):
    r, d = c_rows.shape
    n = w_ada.shape[1]
    tn = 1536
    return pl.pallas_call(
        _ada_kernel,
        grid=(n // tn,),
        in_specs=[pl.BlockSpec((r, d), lambda j: (0, 0)),
                  pl.BlockSpec((d, tn), lambda j: (0, j)),
                  pl.BlockSpec((1, tn), lambda j: (0, j))],
        out_specs=pl.BlockSpec((r, tn), lambda j: (0, j)),
        out_shape=jax.ShapeDtypeStruct((r, n), F32),
        compiler_params=_params(1),
        name="ada",
    )(c_rows, w_ada, b_ada.reshape(1, n))


def _in_proj_kernel(*refs, with_gate, rotate, seq_tiles):
    refs = list(refs)
    take = lambda n: [refs.pop(0) for _ in range(n)]
    x_ref, xp_ref, xn_ref, g_ref, sc_ref, sh_ref = take(6)
    w_na, w_mla, w_gqa, w_xbc, w_z, w_dt = take(6)
    w_gate = take(1)[0] if with_gate else None
    mla_params = take(5)
    gqa_params = take(3)
    cw_ref, cb_ref = take(2)
    mla_tabs = take(2) if rotate else (None, None)
    gqa_tabs = take(2) if rotate else (None, None)
    o_na, q_mla, k_mla, v_mla, q_gqa, k_gqa, v_gqa, o_xbc, o_z, o_dt = take(10)
    tm = x_ref.shape[0]

    def modulated(x):
        return (_rms(x, g_ref[...]) * (1.0 + sc_ref[0]) + sh_ref[0]).astype(BF16)

    hb = modulated(x_ref[...])

    def project(w_ref, o_ref):
        width = w_ref.shape[1]
        for c0 in range(0, width, 512):
            cw = min(512, width - c0)
            o_ref[:, c0:c0 + cw] = _dot(hb, w_ref[:, c0:c0 + cw]).astype(o_ref.dtype)

    project(w_na, o_na)
    _mla_prep(_dot(hb, w_mla[...]), *mla_params, *mla_tabs, q_mla, k_mla, v_mla)
    _gqa_prep(_dot(hb, w_gqa[...]), *gqa_params, *gqa_tabs, q_gqa, k_gqa, v_gqa)

    i = pl.program_id(0) % seq_tiles
    zero = jnp.zeros((CONV_HALO, x_ref.shape[1]), BF16)
    h_prev = jnp.where(i > 0, modulated(xp_ref[...]), zero)
    h_next = jnp.where(i < seq_tiles - 1, modulated(xn_ref[...]), zero)
    h_ext = jnp.concatenate([h_prev, hb, h_next], axis=0)
    rows = tm + 2 * CONV_HALO
    for c0 in range(0, w_xbc.shape[1], 2 * LANES):
        p_ext = _dot(h_ext, w_xbc[:, c0:c0 + 2 * LANES])
        for c1 in range(c0, c0 + 2 * LANES, LANES):
            sl = slice(c1, c1 + LANES)
            xcat = p_ext[:, c1 - c0:c1 - c0 + LANES]
            acc = jnp.zeros((tm, LANES), F32) + cb_ref[:, sl]
            for k in range(SSM_CONV):
                sh = (SSM_CONV // 2 - k) % rows
                rolled = xcat if sh == 0 else pltpu.roll(xcat, sh, axis=0)
                acc = acc + rolled[CONV_HALO:CONV_HALO + tm] * cw_ref[k:k + 1, sl]
            o_xbc[:, sl] = _silu(acc).astype(o_xbc.dtype)

    project(w_z, o_z)
    project(w_dt, o_dt)
    if with_gate:
        project(w_gate, refs.pop(0))


def _in_proj(x2, g, sc, sh, seq_len, per_seq_mod, w, lw, tables, with_gate, tm):
    m, d = x2.shape
    rotate = tables is not None
    seq_tiles = seq_len // tm
    mod_map = (lambda i: (i // seq_tiles, 0, 0)) if per_seq_mod else (lambda i: (0, 0, 0))
    weights = [w['na'], w['mla'], w['gqa'], w['xbc'], w['z'], w['dt']] + ([w['gate']] if with_gate else [])
    bd = jnp.asarray(np.kron(np.eye(2), np.ones((HALF, HALF))), BF16)
    nch = w['xbc'].shape[1]
    conv_w = jnp.zeros((8, nch), F32).at[:SSM_CONV].set(lw['ssm_conv_w'])
    params = [lw['mla_g_q'].reshape(1, -1), lw['mla_g_kv'].reshape(1, -1), w['mla_uq'], w['mla_uk'], w['mla_uv'],
              jnp.tile(lw['gqa_g_q'], 2).reshape(1, LANES), jnp.tile(lw['gqa_g_k'], 2).reshape(1, LANES), bd,
              conv_w, lw['ssm_conv_b'].reshape(1, nch)]
    hb = tm // CONV_HALO
    last = m // CONV_HALO - 1
    in_specs = [pl.BlockSpec((tm, d), lambda i: (i, 0)),
                pl.BlockSpec((CONV_HALO, d), lambda i: (jnp.maximum(i * hb - 1, 0), 0)),
                pl.BlockSpec((CONV_HALO, d), lambda i: (jnp.minimum((i + 1) * hb, last), 0)),
                _resident((1, d)), pl.BlockSpec((1, 1, d), mod_map), pl.BlockSpec((1, 1, d), mod_map)]
    in_specs += [_resident(a.shape) for a in weights + params]
    args = [x2, x2, x2, g.reshape(1, d), sc, sh] + weights + params
    if rotate:
        in_specs += [pl.BlockSpec((tm, LANES), lambda i: (i % seq_tiles, 0))] * 4
        args += list(tables['mla']) + list(tables['gqa'])
    outs = [(w['na'].shape[1], BF16),
            (MLA_HEADS * LANES, BF16), (MLA_HEADS * LANES, BF16), (MLA_HEADS * MLA_V, BF16),
            (GQA_Q_HEADS * GQA_HEAD_DIM, BF16), (GQA_KV_HEADS * QUAD_W, BF16), (GQA_KV_HEADS * QUAD_W, BF16),
            (nch, BF16), (w['z'].shape[1], BF16), (w['dt'].shape[1], F32)]
    if with_gate:
        outs.append((w['gate'].shape[1], BF16))
    return pl.pallas_call(
        functools.partial(_in_proj_kernel, with_gate=with_gate, rotate=rotate, seq_tiles=seq_tiles),
        grid=(m // tm,),
        in_specs=in_specs,
        out_specs=[pl.BlockSpec((tm, n), lambda i: (i, 0)) for n, _ in outs],
        out_shape=[jax.ShapeDtypeStruct((m, n), dt) for n, dt in outs],
        compiler_params=_params(1),
        name="in_proj",
    )(*args)


def _na_kernel(q_ref, k_ref, v_ref, kc_ref, vc_ref, bias_ref, o_ref):
    lo = _lane() < HALF
    kc = kc_ref[...]
    vc = vc_ref[...]
    qscale = NA_HEAD_DIM ** -0.5 * LOG2E
    nq = NA_BLK * GRID_W
    nk = NA_WIN_ROWS * GRID_W
    for bi in range(NA_BLOCKS):
        r0, ws = bi * NA_BLK, _na_window_start(bi)
        q = q_ref[r0 * GRID_W:r0 * GRID_W + nq, :]
        q = (q.astype(F32) * qscale).astype(q.dtype)
        kw = k_ref[ws * GRID_W:ws * GRID_W + nk, :]
        vw = v_ref[ws * GRID_W:ws * GRID_W + nk, :]
        zero = jnp.zeros_like(q)
        q2 = jnp.concatenate([jnp.where(lo, q, zero), jnp.where(lo, zero, q)], axis=0)
        s_loc = _dot_nt(q2, kw) + bias_ref[0, _na_block_kind(bi)]
        s_ctx = _dot_nt(q2, kc)
        m = jnp.maximum(jnp.max(s_loc, axis=-1, keepdims=True), jnp.max(s_ctx, axis=-1, keepdims=True))
        p_loc = jnp.exp2(s_loc - m)
        p_ctx = jnp.exp2(s_ctx - m)
        den = jnp.sum(p_loc, axis=-1, keepdims=True) + jnp.sum(p_ctx, axis=-1, keepdims=True)
        o = (_dot(p_loc.astype(BF16), vw) + _dot(p_ctx.astype(BF16), vc)) / den
        o_ref[r0 * GRID_W:r0 * GRID_W + nq, :] = jnp.where(lo, o[:nq], o[nq:]).astype(o_ref.dtype)


def _na_window_start(bi):
    return min(max(bi * NA_BLK - NA_ROW_WIN // 2, 0), GRID_ROWS - NA_WIN_ROWS)


def _na_block_kind(bi):
    return 0 if bi == 0 else (2 if bi == NA_BLOCKS - 1 else 1)


def _na_bias_table(rpb_all):
    n_heads = rpb_all.shape[0] * NA_HEADS
    n_rho = 2 * NA_ROW_WIN - 1
    n_off = 2 * NA_COL_WIN - 1
    w = np.arange(GRID_W)
    j = np.arange(GRID_W)
    cs = np.clip(w - NA_COL_WIN // 2, 0, GRID_W - NA_COL_WIN)
    valid = (j[None, :] >= cs[:, None]) & (j[None, :] < cs[:, None] + NA_COL_WIN)
    r = rpb_all.astype(F32).reshape(n_heads, n_rho, n_off) * LOG2E
    r = jnp.pad(r, ((0, 0), (0, 0), (0, GRID_W)))
    period = n_off + GRID_W
    skew = jnp.broadcast_to(r[:, :, None, :], (n_heads, n_rho, GRID_W, period)).reshape(n_heads, n_rho, GRID_W * period)
    skew = skew[:, :, :GRID_W * (period - 1)].reshape(n_heads, n_rho, GRID_W, period - 1)
    t = jnp.where(valid[None, None], skew[:, :, :, NA_COL_WIN - 1:NA_COL_WIN - 1 + GRID_W], NEG)
    tt = t.transpose(0, 2, 1, 3).reshape(n_heads, GRID_W, n_rho * GRID_W)
    ttp = jnp.pad(tt, ((0, 0), (0, 0), (NA_BLK * GRID_W, NA_BLK * GRID_W)), constant_values=NEG)
    nk = NA_WIN_ROWS * GRID_W
    patterns = {}
    for bi in range(NA_BLOCKS):
        ws = _na_window_start(bi)
        pat = []
        for u in range(NA_BLK):
            r = bi * NA_BLK + u
            rs = min(max(r - NA_ROW_WIN // 2, 0), GRID_ROWS - NA_ROW_WIN)
            rho0 = ws - r + NA_ROW_WIN - 1
            in_win = tuple(0 <= ws + i - rs < NA_ROW_WIN for i in range(NA_WIN_ROWS))
            pat.append((rho0, in_win))
        assert patterns.setdefault(_na_block_kind(bi), pat) == pat
    kinds = []
    for kind in range(3):
        per_u = []
        for rho0, in_win in patterns[kind]:
            start = (rho0 + NA_BLK) * GRID_W
            mask = np.repeat(np.asarray(in_win), GRID_W)
            per_u.append(jnp.where(mask[None, None, :], ttp[:, :, start:start + nk], NEG))
        kinds.append(jnp.stack(per_u, axis=1).reshape(n_heads // 2, 2 * NA_BLK * GRID_W, nk))
    return jnp.stack(kinds, axis=1)


def _na_attention(o_na_l, o_na_c, bias, layer, nb):
    return pl.pallas_call(
        _na_kernel,
        grid=(nb, N_PAIRS),
        in_specs=[pl.BlockSpec((SEQ, LANES), lambda b, p: (b, p)),
                  pl.BlockSpec((SEQ, LANES), lambda b, p: (b, N_PAIRS + p)),
                  pl.BlockSpec((SEQ, LANES), lambda b, p: (b, 2 * N_PAIRS + p)),
                  pl.BlockSpec((CTX_LEN, LANES), lambda b, p: (b, N_PAIRS + p)),
                  pl.BlockSpec((CTX_LEN, LANES), lambda b, p: (b, 2 * N_PAIRS + p)),
                  pl.BlockSpec((1, 3, 2 * NA_BLK * GRID_W, NA_WIN_ROWS * GRID_W), lambda b, p: (layer * N_PAIRS + p, 0, 0, 0))],
        out_specs=pl.BlockSpec((SEQ, LANES), lambda b, p: (b, p)),
        out_shape=jax.ShapeDtypeStruct((nb * SEQ, NA_HEADS * NA_HEAD_DIM), BF16),
        compiler_params=_params(2),
        name="na_attn",
    )(o_na_l, o_na_l, o_na_l, o_na_c, o_na_c, bias)


def _attn_kernel(*refs, n_src, dq, scale):
    q_ref = refs[0]
    o_ref = refs[-1]
    lo = _lane() < HALF
    lane_o = _lane((1, QUAD_W))
    for quad in range(ATTN_HEADS // QUAD):
        vsl = slice(quad * QUAD_W, (quad + 1) * QUAD_W)
        outs = []
        for a in range(quad * QUAD, (quad + 1) * QUAD):
            c0 = a * LANES if dq == LANES else (a // 2) * LANES
            qa = q_ref[:, c0:c0 + LANES]
            if scale != 1.0:
                qa = (qa.astype(F32) * scale).astype(qa.dtype)
            if dq != LANES:
                zero = jnp.zeros_like(qa)
                qa = jnp.where(lo, qa, zero) if a % 2 == 0 else jnp.where(lo, zero, qa)
            ss = [_dot_nt(qa, refs[1 + 2 * s][:, c0:c0 + LANES]) for s in range(n_src)]
            m = ss[0].max(axis=-1, keepdims=True)
            for sc in ss[1:]:
                m = jnp.maximum(m, sc.max(axis=-1, keepdims=True))
            den = None
            o = None
            for s in range(n_src):
                p = jnp.exp2(ss[s] - m)
                ps = jnp.sum(p, axis=-1, keepdims=True)
                po = _dot(p.astype(BF16), refs[2 + 2 * s][:, vsl])
                den = ps if den is None else den + ps
                o = po if o is None else o + po
            outs.append(o / den)
        out = outs[QUAD - 1]
        for a in range(QUAD - 2, -1, -1):
            out = jnp.where(lane_o < (a + 1) * HALF, outs[a], out)
        o_ref[:, vsl] = out.astype(o_ref.dtype)


def _attention(q, srcs, *, nb, t, dq, tq, scale=1.0):
    nq = t // tq
    in_specs = [pl.BlockSpec((tq, ATTN_HEADS * dq), lambda b, i: (b * nq + i, 0))]
    args = [q]
    for k_arr, v_arr, u, k_off, v_off in srcs:
        in_specs.append(pl.BlockSpec((u, ATTN_HEADS * dq), lambda b, i, k_off=k_off: (b, k_off)))
        in_specs.append(pl.BlockSpec((u, ATTN_HEADS * HALF), lambda b, i, v_off=v_off: (b, v_off)))
        args += [k_arr, v_arr]
    return pl.pallas_call(
        functools.partial(_attn_kernel, n_src=len(srcs), dq=dq, scale=scale),
        grid=(nb, nq),
        in_specs=in_specs,
        out_specs=pl.BlockSpec((tq, ATTN_HEADS * HALF), lambda b, i: (b * nq + i, 0)),
        out_shape=jax.ShapeDtypeStruct((nb * t, ATTN_HEADS * HALF), BF16),
        compiler_params=_params(2),
        name="attn",
    )(*args)


def _mla_prep(x, gq_ref, gkv_ref, wq_ref, wk_ref, wv_ref, cos_ref, sin_ref, q_ref, k_ref, v_ref):
    rotate = cos_ref is not None
    scale = (MLA_NOPE + MLA_ROPE) ** -0.5 * LOG2E
    cq = x[:, :MLA_Q_RANK]
    ckv = x[:, MLA_Q_RANK:MLA_Q_RANK + MLA_KV_RANK]
    kr = x[:, MLA_Q_RANK + MLA_KV_RANK:]
    cqn = _rms(cq, gq_ref[...]).astype(BF16)
    ckvn = _rms(ckv, gkv_ref[...]).astype(BF16)
    lane = _lane()
    first_half = lane < MLA_NOPE + MLA_ROPE // 2

    def rope(t):
        if not rotate:
            return t
        rot = jnp.where(first_half, pltpu.roll(t, LANES - MLA_ROPE // 2, axis=1), pltpu.roll(t, MLA_ROPE // 2, axis=1))
        return t * cos_ref[...] + rot * sin_ref[...]

    kr = rope(kr)
    for h in range(MLA_HEADS):
        sl = slice(h * LANES, (h + 1) * LANES)
        qh = rope(_dot(cqn, wq_ref[:, sl]))
        q_ref[:, sl] = (qh * scale).astype(q_ref.dtype)
        k_ref[:, sl] = (_dot(ckvn, wk_ref[:, sl]) + kr).astype(k_ref.dtype)
    v_ref[...] = _dot(ckvn, wv_ref[...]).astype(v_ref.dtype)


def _gqa_prep(x, gq_ref, gk_ref, bd_ref, cos_ref, sin_ref, q_ref, k_ref, v_ref):
    rotate = cos_ref is not None
    scale = GQA_HEAD_DIM ** -0.5 * LOG2E
    lane = _lane()
    lo = lane < HALF
    first_half = (lane & (GQA_HEAD_DIM - 1)) < GQA_HEAD_DIM // 2

    def headnorm(t, g):
        ss = _dot((t * t).astype(BF16), bd_ref[...]) * (1.0 / GQA_HEAD_DIM)
        return t * lax.rsqrt(ss + EPS) * g

    def rope(t):
        if not rotate:
            return t
        rot = jnp.where(first_half, pltpu.roll(t, LANES - GQA_HEAD_DIM // 2, axis=1), pltpu.roll(t, GQA_HEAD_DIM // 2, axis=1))
        return t * cos_ref[...] + rot * sin_ref[...]

    for j in range(N_PAIRS):
        sl = slice(j * LANES, (j + 1) * LANES)
        q_ref[:, sl] = (rope(headnorm(x[:, sl], gq_ref[...])) * scale).astype(q_ref.dtype)
    nq = GQA_Q_HEADS * GQA_HEAD_DIM
    k = rope(headnorm(x[:, nq:nq + LANES], gk_ref[...]))
    v = x[:, nq + LANES:nq + 2 * LANES]
    for t, ref in ((k, k_ref), (v, v_ref)):
        sw = pltpu.roll(t, HALF, axis=1)
        for g, dup in enumerate((jnp.where(lo, t, sw), jnp.where(lo, sw, t))):
            dup = dup.astype(ref.dtype)
            ref[:, (2 * g) * LANES:(2 * g + 1) * LANES] = dup
            ref[:, (2 * g + 1) * LANES:(2 * g + 2) * LANES] = dup


def _rope_tables(dim, lane_start, group):
    t = jnp.arange(SEQ)
    row = (t // GRID_W).astype(F32)
    col = (t % GRID_W).astype(F32)
    quarter = dim // 4
    inv = ROPE_THETA ** (-jnp.arange(quarter, dtype=F32) / quarter)
    ang = jnp.concatenate([row[:, None] * inv, col[:, None] * inv], axis=-1)
    cos, sin = jnp.cos(ang), jnp.sin(ang)
    cos_g = jnp.ones((SEQ, group), F32).at[:, lane_start:lane_start + dim].set(jnp.concatenate([cos, cos], axis=-1))
    sin_g = jnp.zeros((SEQ, group), F32).at[:, lane_start:lane_start + dim].set(jnp.concatenate([-sin, sin], axis=-1))
    reps = LANES // group
    return jnp.tile(cos_g, (1, reps)), jnp.tile(sin_g, (1, reps))


def _softplus(x):
    return jnp.maximum(x, 0.0) + jnp.log(1.0 + jnp.exp(-jnp.abs(x)))


def _ssd_kernel(*refs, with_y):
    n_dir = 4 if with_y else 3
    dir_refs = (refs[:n_dir], refs[n_dir:2 * n_dir])
    al_ref, bi_ref, h0_ref = refs[2 * n_dir:2 * n_dir + 3]
    if with_y:
        yf_ref, yb_ref, hf_ref, st = refs[2 * n_dir + 3:]
        y_refs = (yf_ref, yb_ref)
    else:
        hf_ref, st = refs[2 * n_dir + 3:]
    j = pl.program_id(1)
    q = SSM_CHUNK

    @pl.when(j == 0)
    def _():
        st[...] = h0_ref[:, 0]

    row = lax.broadcasted_iota(jnp.int32, (q, q), 0)
    col = lax.broadcasted_iota(jnp.int32, (q, q), 1)
    lane = _lane()
    lo = lane < HALF

    def col_of(x, h):
        return jnp.sum(jnp.where(lane == h, x, 0.0), axis=1, keepdims=True)

    for d in range(2):
        if with_y:
            x_ref, b_ref, c_ref, dt_ref = dir_refs[d]
        else:
            x_ref, b_ref, dt_ref = dir_refs[d]
        m_qs = (col <= row) if d == 0 else (col >= row)
        dt = _softplus(dt_ref[...] + bi_ref[d])
        da = jnp.where(lane < SSM_HEADS, dt * (-jnp.exp(al_ref[d])), 0.0)
        a_cum_c = jnp.dot(jnp.where(m_qs, 1.0, 0.0), da, precision=HIGHEST, preferred_element_type=F32)
        a_tot = jnp.sum(da, axis=0, keepdims=True)
        a_cum_r = a_cum_c.T

        x = x_ref[...].astype(F32)
        if with_y:
            cm = c_ref[...]
        for g in range(SSM_GROUPS):
            btg = b_ref[:, g * SSM_STATE:(g + 1) * SSM_STATE].astype(F32).T.astype(BF16)
            if with_y:
                cg = cm[:, g * SSM_STATE:(g + 1) * SSM_STATE]
                cb = _dot(cg, btg)
            for pp in range(2):
                p = 2 * g + pp
                h0, h1 = 2 * p, 2 * p + 1
                ac0, ac1 = col_of(a_cum_c, h0), col_of(a_cum_c, h1)
                tot0, tot1 = col_of(a_tot, h0), col_of(a_tot, h1)
                xdt = x[:, p * LANES:(p + 1) * LANES] * jnp.where(lo, col_of(dt, h0), col_of(dt, h1))
                ht = st[d, p]
                if with_y:
                    xdt_b = xdt.astype(BF16)
                    l0 = jnp.exp(jnp.where(m_qs, ac0 - a_cum_r[h0:h0 + 1, :], NEG))
                    l1 = jnp.exp(jnp.where(m_qs, ac1 - a_cum_r[h1:h1 + 1, :], NEG))
                    y_in = jnp.where(lo, _dot((cb * l0).astype(BF16), xdt_b), _dot((cb * l1).astype(BF16), xdt_b))
                    y_st = _dot(cg, ht.astype(BF16)) * jnp.where(lo, jnp.exp(ac0), jnp.exp(ac1))
                    y_refs[d][:, p * LANES:(p + 1) * LANES] = (y_in + y_st).astype(y_refs[d].dtype)
                dec = jnp.where(lo, jnp.exp(tot0 - ac0), jnp.exp(tot1 - ac1))
                st[d, p] = ht * jnp.where(lo, jnp.exp(tot0), jnp.exp(tot1)) + _dot(btg, (xdt * dec).astype(BF16))

    @pl.when(j == pl.num_programs(1) - 1)
    def _():
        hf_ref[:, 0] = st[...]


def _ssd(xbc, o_dt, a_log, dt_bias, h0, *, nb, t, with_y):
    nc = t // SSM_CHUNK
    pad = LANES - SSM_HEADS
    al_row = jnp.pad(a_log.astype(F32), ((0, 0), (0, pad))).reshape(2, 1, LANES)
    bi_row = jnp.pad(dt_bias.astype(F32), ((0, 0), (0, pad))).reshape(2, 1, LANES)
    x_blk = SSM_INNER // SSM_BC
    in_specs, args = [], []
    for d in range(2):
        cidx = (lambda j: j) if d == 0 else (lambda j: nc - 1 - j)
        in_specs += [pl.BlockSpec((SSM_CHUNK, SSM_INNER), lambda b, j, c=cidx: (b * nc + c(j), 0)),
                     pl.BlockSpec((SSM_CHUNK, SSM_BC), lambda b, j, c=cidx: (b * nc + c(j), x_blk))]
        args += [xbc, xbc]
        if with_y:
            in_specs.append(pl.BlockSpec((SSM_CHUNK, SSM_BC), lambda b, j, c=cidx: (b * nc + c(j), x_blk + 1)))
            args.append(xbc)
        in_specs.append(pl.BlockSpec((SSM_CHUNK, LANES), lambda b, j, c=cidx, d=d: (b * nc + c(j), d)))
        args.append(o_dt)
    st_spec = pl.BlockSpec((2, 1, N_PAIRS, SSM_STATE, LANES), lambda b, j: (0, b, 0, 0, 0))
    in_specs += [_resident(al_row.shape), _resident(bi_row.shape), st_spec]
    args += [al_row, bi_row, h0]
    st_shape = jax.ShapeDtypeStruct((2, nb, N_PAIRS, SSM_STATE, LANES), F32)
    out_specs, out_shape = [st_spec], [st_shape]
    if with_y:
        out_specs = [pl.BlockSpec((SSM_CHUNK, SSM_INNER), lambda b, j: (b * nc + j, 0)),
                     pl.BlockSpec((SSM_CHUNK, SSM_INNER), lambda b, j: (b * nc + nc - 1 - j, 0))] + out_specs
        out_shape = [jax.ShapeDtypeStruct((nb * t, SSM_INNER), BF16)] * 2 + out_shape
    return pl.pallas_call(
        functools.partial(_ssd_kernel, with_y=with_y),
        grid=(nb, nc),
        in_specs=in_specs, out_specs=out_specs, out_shape=out_shape,
        scratch_shapes=[pltpu.VMEM((2, N_PAIRS, SSM_STATE, LANES), F32)],
        compiler_params=_params(2),
        name="ssd",
    )(*args)


def _merge_kernel(yna_ref, ymla_ref, ygqa_ref, yf_ref, yb_ref, xs_ref, z_ref, dsk_ref, gssm_ref,
                  gate_ref, wb_ref, wo_ref, x_ref, gt_ref, g_ref, o_ref):
    y = yf_ref[...].astype(F32) + yb_ref[...].astype(F32) + dsk_ref[...] * xs_ref[...].astype(F32)
    y_ssm = _rms(y * _silu(z_ref[...].astype(F32)), gssm_ref[...]).astype(BF16)
    acc = None
    for k, yk in enumerate((yna_ref[...], ymla_ref[...], ygqa_ref[...], y_ssm)):
        proj = _dot(yk, wb_ref[k])
        gate = jax.nn.sigmoid(gate_ref[:, k * D_MODEL:(k + 1) * D_MODEL].astype(F32))
        acc = gate * proj if acc is None else acc + gate * proj
    mix = _dot(acc.astype(BF16), wo_ref[...])
    o_ref[...] = x_ref[...] + gt_ref[0] * _rms(mix, g_ref[...])


def _merge(ys, y_f, y_b, xbc, z, d_skip, g_norm, gate, wb, wo, x2, gt, g_post, tiles_per_mod, tm=512):
    m, d = x2.shape
    mod_map = (lambda i: (0, 0, 0)) if tiles_per_mod is None else (lambda i: (i // tiles_per_mod, 0, 0))
    dvec = jnp.repeat(d_skip.astype(F32), SSM_HEAD_DIM).reshape(1, SSM_INNER)
    row_blk = pl.BlockSpec((tm, BRANCH_W), lambda i: (i, 0))
    in_specs = [row_blk] * (N_BRANCH - 1) + [row_blk, row_blk, row_blk, row_blk,
                                             _resident((1, SSM_INNER)), _resident((1, SSM_INNER))]
    in_specs += [pl.BlockSpec((tm, N_BRANCH * d), lambda i: (i, 0)),
                 _resident(wb.shape), _resident(wo.shape),
                 pl.BlockSpec((tm, d), lambda i: (i, 0)),
                 pl.BlockSpec((1, 1, d), mod_map), _resident((1, d))]
    return pl.pallas_call(
        _merge_kernel,
        grid=(m // tm,),
        in_specs=in_specs,
        out_specs=pl.BlockSpec((tm, d), lambda i: (i, 0)),
        out_shape=jax.ShapeDtypeStruct((m, d), F32),
        compiler_params=_params(1),
        name="merge",
    )(*ys, y_f, y_b, xbc, z, dvec, g_norm.reshape(1, SSM_INNER), gate, wb, wo, x2, gt, g_post.reshape(1, d))


def _ffn_kernel(x_ref, gpre_ref, sc_ref, sh_ref, w1_ref, w3_ref, w2_ref, gt_ref, gpost_ref, o_ref):
    x = x_ref[...]
    hb = (_rms(x, gpre_ref[...]) * (1.0 + sc_ref[0]) + sh_ref[0]).astype(BF16)
    u = _silu(_dot(hb, w1_ref[...])) * _dot(hb, w3_ref[...])
    f = _dot(u.astype(BF16), w2_ref[...])
    o_ref[...] = x + gt_ref[0] * _rms(f, gpost_ref[...])


def _ffn(x2, g_pre, sc, sh, w1, w3, w2, gt, g_post, tiles_per_mod, tm=512):
    m, d = x2.shape
    mod_map = (lambda i: (0, 0, 0)) if tiles_per_mod is None else (lambda i: (i // tiles_per_mod, 0, 0))
    mod = pl.BlockSpec((1, 1, d), mod_map)
    return pl.pallas_call(
        _ffn_kernel,
        grid=(m // tm,),
        in_specs=[pl.BlockSpec((tm, d), lambda i: (i, 0)), _resident((1, d)), mod, mod,
                  _resident(w1.shape), _resident(w3.shape), _resident(w2.shape), mod, _resident((1, d))],
        out_specs=pl.BlockSpec((tm, d), lambda i: (i, 0)),
        out_shape=jax.ShapeDtypeStruct((m, d), F32),
        compiler_params=_params(1),
        name="ffn",
    )(x2, g_pre.reshape(1, d), sc, sh, w1, w3, w2, gt, g_post.reshape(1, d))


def _split_w_in(w_in):
    off, cols = 0, {}
    for name, w in KEY_SPLITS + QUERY_SPLITS:
        cols[name] = w_in[:, off:off + w]
        off += w
    d = w_in.shape[0]
    z = lambda n: jnp.zeros((d, n), w_in.dtype)
    cat = lambda parts: jnp.concatenate(parts, axis=1).astype(BF16)
    w_na = cat([cols['na_q'], cols['na_k'], cols['na_v']])
    w_mla = cat([cols['mla_cq'], cols['mla_ckv'], z(MLA_NOPE), cols['mla_kr'], z(LANES - MLA_NOPE - MLA_ROPE)])
    w_gqa = cat([cols['gqa_q'], cols['gqa_k'], cols['gqa_v']])
    w_xbc = cat([cols['ssm_x'], cols['ssm_B'], cols['ssm_C']])
    w_z = cols['ssm_z'].astype(BF16)
    w_dt = cat([cols['ssm_dt'][:, :SSM_HEADS], z(LANES - SSM_HEADS), cols['ssm_dt'][:, SSM_HEADS:], z(LANES - SSM_HEADS)])
    w_gate = w_in[:, MIX_COLS:].astype(BF16)
    return dict(na=w_na, mla=w_mla, gqa=w_gqa, xbc=w_xbc, z=w_z, dt=w_dt, gate=w_gate)


def _layer(x2, xc2, mods, mc, lw, tables, layer, nb, need_ctx):
    d = D_MODEL
    tpm = SEQ // 512
    chunk = lambda a, k: a[:, k * d:(k + 1) * d].reshape(-1, 1, d)
    sh1, sc1, gt1, sh2, sc2, gt2 = (chunk(mods, k) for k in range(6))
    csh1, csc1, cgt1, csh2, csc2, cgt2 = (chunk(mc, k) for k in range(6))
    w = _split_w_in(lw['w_in'])
    w['mla_uq'] = jnp.pad(lw['mla_w_uq'].reshape(MLA_Q_RANK, MLA_HEADS, MLA_NOPE + MLA_ROPE),
                          ((0, 0), (0, 0), (0, LANES - MLA_NOPE - MLA_ROPE))).reshape(MLA_Q_RANK, MLA_HEADS * LANES)
    wkv = lw['mla_w_ukv'].reshape(MLA_KV_RANK, MLA_HEADS, MLA_NOPE + MLA_V)
    w['mla_uk'] = jnp.pad(wkv[:, :, :MLA_NOPE], ((0, 0), (0, 0), (0, LANES - MLA_NOPE))).reshape(MLA_KV_RANK, MLA_HEADS * LANES)
    w['mla_uv'] = wkv[:, :, MLA_NOPE:].reshape(MLA_KV_RANK, MLA_HEADS * MLA_V)
    o_na, ql, kl, vl, gql, gkl, gvl, xbc_l, z_l, o_dt, gate = _in_proj(
        x2, lw['g_pre1'], sc1, sh1, SEQ, True, w, lw, tables, True, IN_PROJ_TM)
    res = _in_proj(xc2, lw['g_pre1'], csc1, csh1, CTX_LEN, False, w, lw, None, need_ctx, CTX_LEN)
    c_na, qc, kc, vc, gqc, gkc, gvc, xbc_c, z_c, c_dt = res[:10]
    c_gate = res[10] if need_ctx else None

    y_na = _na_attention(o_na, c_na, tables['na_bias'], layer, nb)
    y_mla = _attention(ql, [(kc, vc, CTX_LEN, 0, 0), (kl, vl, SEQ, 0, 0)], nb=nb, t=SEQ, dq=LANES, tq=256)
    y_gqa = _attention(gql, [(gkc, gvc, CTX_LEN, 0, 0), (gkl, gvl, SEQ, 0, 0)], nb=nb, t=SEQ, dq=HALF, tq=256)

    h_zero = jnp.zeros((2, nb, N_PAIRS, SSM_STATE, LANES), F32)
    res_c = _ssd(xbc_c, c_dt, lw['ssm_a_log'], lw['ssm_dt_bias'], h_zero, nb=nb, t=CTX_LEN, with_y=need_ctx)
    h_ctx = res_c[-1]
    yf_l, yb_l, _ = _ssd(xbc_l, o_dt, lw['ssm_a_log'], lw['ssm_dt_bias'], h_ctx, nb=nb, t=SEQ, with_y=True)

    wb = lw['w_branch'].astype(BF16)
    wo = lw['w_out'].astype(BF16)
    w1, w3, w2 = lw['ffn_w1'].astype(BF16), lw['ffn_w3'].astype(BF16), lw['ffn_w2'].astype(BF16)
    x2 = _merge((y_na, y_mla, y_gqa), yf_l, yb_l, xbc_l, z_l, lw['ssm_d'], lw['ssm_g_norm'],
                gate, wb, wo, x2, gt1, lw['g_post1'], tpm)
    x2 = _ffn(x2, lw['g_pre2'], sc2, sh2, w1, w3, w2, gt2, lw['g_post2'], tpm)

    if need_ctx:
        yc_na = _attention(c_na, [(c_na, c_na, CTX_LEN, 1, 2)], nb=nb, t=CTX_LEN, dq=HALF,
                           tq=CTX_LEN, scale=NA_HEAD_DIM ** -0.5 * LOG2E)
        yc_mla = _attention(qc, [(kc, vc, CTX_LEN, 0, 0)], nb=nb, t=CTX_LEN, dq=LANES, tq=CTX_LEN)
        yc_gqa = _attention(gqc, [(gkc, gvc, CTX_LEN, 0, 0)], nb=nb, t=CTX_LEN, dq=HALF, tq=CTX_LEN)
        xc2 = _merge((yc_na, yc_mla, yc_gqa), res_c[0], res_c[1], xbc_c, z_c, lw['ssm_d'], lw['ssm_g_norm'],
                     c_gate, wb, wo, xc2, cgt1, lw['g_post1'], None)
        xc2 = _ffn(xc2, lw['g_pre2'], csc2, csh2, w1, w3, w2, cgt2, lw['g_post2'], None)
    return x2, xc2


def kernel(x, c, ctx, c_ctx, w_ada, b_ada, g_pre1, g_post1, g_pre2, g_post2, w_in, na_rpb, mla_g_q, mla_g_kv, mla_w_uq, mla_w_ukv, gqa_g_q, gqa_g_k, ssm_conv_w, ssm_conv_b, ssm_a_log, ssm_dt_bias, ssm_d, ssm_g_norm, w_branch, w_out, ffn_w1, ffn_w3, ffn_w2):
    nb, s, d = x.shape
    assert s == SEQ and d == D_MODEL and ctx.shape[1] == CTX_LEN
    depth = w_ada.shape[0]
    tables = {'mla': _rope_tables(MLA_ROPE, MLA_NOPE, LANES), 'gqa': _rope_tables(GQA_HEAD_DIM, 0, GQA_HEAD_DIM),
              'na_bias': _na_bias_table(na_rpb)}
    x2 = x.reshape(nb * s, d)
    xc2 = ctx.reshape(nb * CTX_LEN, d)
    n_rows = -(-(nb + 1) // 8) * 8
    c_rows = jnp.zeros((n_rows, d), F32).at[:nb].set(c).at[nb].set(c_ctx)
    stacked = dict(w_in=w_in, na_rpb=na_rpb, mla_g_q=mla_g_q, mla_g_kv=mla_g_kv, mla_w_uq=mla_w_uq,
                   mla_w_ukv=mla_w_ukv, gqa_g_q=gqa_g_q, gqa_g_k=gqa_g_k, ssm_conv_w=ssm_conv_w,
                   ssm_conv_b=ssm_conv_b, ssm_a_log=ssm_a_log, ssm_dt_bias=ssm_dt_bias, ssm_d=ssm_d,
                   ssm_g_norm=ssm_g_norm, w_branch=w_branch, w_out=w_out, ffn_w1=ffn_w1, ffn_w3=ffn_w3,
                   ffn_w2=ffn_w2, g_pre1=g_pre1, g_post1=g_post1, g_pre2=g_pre2, g_post2=g_post2)
    for k in ('w_in', 'mla_w_uq', 'mla_w_ukv', 'w_branch', 'w_out', 'ffn_w1', 'ffn_w3', 'ffn_w2'):
        stacked[k] = stacked[k].astype(BF16)
    for l in range(depth):
        lw = {k: v[l] for k, v in stacked.items()}
        mod_all = _ada(c_rows, w_ada[l], b_ada[l])
        x2, xc2 = _layer(x2, xc2, mod_all[:nb], mod_all[nb:nb + 1], lw, tables, l, nb, l < depth - 1)
    return x2.reshape(nb, s, d)
```

```python
import functools

import numpy as np
import jax
import jax.numpy as jnp
from jax import lax
from jax.experimental import pallas as pl
from jax.experimental.pallas import tpu as pltpu

F32 = jnp.float32
BF16 = jnp.bfloat16
HIGHEST = lax.Precision.HIGHEST

D_MODEL = 1024
SEQ = 2048
GRID_W = 64
GRID_ROWS = SEQ // GRID_W
CTX_LEN = 256
N_BRANCH = 4
BRANCH_W = 512
ROPE_THETA = 10000.0
EPS = 1e-6
NA_HEADS = 8
NA_HEAD_DIM = 64
NA_ROW_WIN = 8
NA_COL_WIN = 16
MLA_HEADS = 8
MLA_Q_RANK = 384
MLA_KV_RANK = 256
MLA_NOPE = 64
MLA_ROPE = 32
MLA_V = 64
GQA_Q_HEADS = 8
GQA_KV_HEADS = 2
GQA_HEAD_DIM = 64
SSM_HEADS = 8
SSM_HEAD_DIM = 64
SSM_GROUPS = 2
SSM_STATE = 128
SSM_CONV = 5
SSM_CHUNK = 128
SSM_INNER = SSM_HEADS * SSM_HEAD_DIM
SSM_BC = SSM_GROUPS * SSM_STATE
FFN_HIDDEN = ((8 * D_MODEL + 3 * 256 - 1) // (3 * 256)) * 256

KEY_SPLITS = (('na_k', 512), ('na_v', 512), ('mla_ckv', MLA_KV_RANK), ('mla_kr', MLA_ROPE),
              ('gqa_k', 128), ('gqa_v', 128), ('ssm_x', SSM_INNER), ('ssm_B', SSM_BC), ('ssm_dt', 2 * SSM_HEADS))
QUERY_SPLITS = (('na_q', 512), ('mla_cq', MLA_Q_RANK), ('gqa_q', 512), ('ssm_C', SSM_BC), ('ssm_z', SSM_INNER))
MIX_COLS = sum(w for _, w in KEY_SPLITS + QUERY_SPLITS)

LANES = 128
HALF = 64
N_PAIRS = 4
NEG = -1e30
LOG2E = 1.4426950408889634
NA_BLK = 4
NA_BLOCKS = GRID_ROWS // NA_BLK
NA_WIN_ROWS = 12
ATTN_HEADS = 8
QUAD = 4
QUAD_W = QUAD * HALF
VMEM_LIMIT = 58 * 1024 * 1024
IN_PROJ_TM = 512
MERGE_TM = 512
CONV_HALO = 16


def _params(n_axes):
    return pltpu.CompilerParams(dimension_semantics=("arbitrary",) * n_axes, vmem_limit_bytes=VMEM_LIMIT)


def _resident(shape):
    nd = len(shape)
    return pl.BlockSpec(shape, lambda *_: (0,) * nd, pipeline_mode=pl.Buffered(1))


def _dot(a, b):
    return jnp.dot(a, b, preferred_element_type=F32)


def _dot_nt(a, b):
    return lax.dot_general(a, b, (((1,), (1,)), ((), ())), preferred_element_type=F32)


def _silu(x):
    return x * jax.nn.sigmoid(x)


def _rms(x, g):
    return x * lax.rsqrt(jnp.mean(x * x, axis=-1, keepdims=True) + EPS) * g


def _lane(shape=(1, LANES)):
    return lax.broadcasted_iota(jnp.int32, shape, len(shape) - 1)


def _ada_kernel(c_ref, w_ref, b_ref, o_ref):
    a = _silu(c_ref[...]).astype(BF16)
    o_ref[...] = _dot(a, w_ref[...].astype(BF16)) + b_ref[...]


def _ada(c_rows, w_ada, b_ada):
    r, d = c_rows.shape
    n = w_ada.shape[1]
    tn = 1536
    return pl.pallas_call(
        _ada_kernel,
        grid=(n // tn,),
        in_specs=[pl.BlockSpec((r, d), lambda j: (0, 0)),
                  pl.BlockSpec((d, tn), lambda j: (0, j)),
                  pl.BlockSpec((1, tn), lambda j: (0, j))],
        out_specs=pl.BlockSpec((r, tn), lambda j: (0, j)),
        out_shape=jax.ShapeDtypeStruct((r, n), F32),
        compiler_params=_params(1),
        name="ada",
    )(c_rows, w_ada, b_ada.reshape(1, n))


def _in_proj_kernel(*refs, with_gate, rotate, seq_tiles):
    refs = list(refs)
    take = lambda n: [refs.pop(0) for _ in range(n)]
    x_ref, xp_ref, xn_ref, g_ref, sc_ref, sh_ref = take(6)
    w_na, w_mla, w_gqa, w_xbc, w_z, w_dt = take(6)
    w_gate = take(1)[0] if with_gate else None
    mla_params = take(5)
    gqa_params = take(3)
    cw_ref, cb_ref = take(2)
    mla_tabs = take(2) if rotate else (None, None)
    gqa_tabs = take(2) if rotate else (None, None)
    o_na, q_mla, k_mla, v_mla, q_gqa, k_gqa, v_gqa, o_xbc, o_z, o_dt = take(10)
    tm = x_ref.shape[0]

    def modulated(x):
        return (_rms(x, g_ref[...]) * (1.0 + sc_ref[0]) + sh_ref[0]).astype(BF16)

    hb = modulated(x_ref[...])

    def project(w_ref, o_ref):
        width = w_ref.shape[1]
        for c0 in range(0, width, 512):
            cw = min(512, width - c0)
            o_ref[:, c0:c0 + cw] = _dot(hb, w_ref[:, c0:c0 + cw]).astype(o_ref.dtype)

    project(w_na, o_na)
    _mla_prep(_dot(hb, w_mla[...]), *mla_params, *mla_tabs, q_mla, k_mla, v_mla)
    _gqa_prep(_dot(hb, w_gqa[...]), *gqa_params, *gqa_tabs, q_gqa, k_gqa, v_gqa)

    i = pl.program_id(0) % seq_tiles
    zero = jnp.zeros((CONV_HALO, x_ref.shape[1]), BF16)
    h_prev = jnp.where(i > 0, modulated(xp_ref[...]), zero)
    h_next = jnp.where(i < seq_tiles - 1, modulated(xn_ref[...]), zero)
    h_ext = jnp.concatenate([h_prev, hb, h_next], axis=0)
    rows = tm + 2 * CONV_HALO
    for c0 in range(0, w_xbc.shape[1], 2 * LANES):
        p_ext = _dot(h_ext, w_xbc[:, c0:c0 + 2 * LANES])
        for c1 in range(c0, c0 + 2 * LANES, LANES):
            sl = slice(c1, c1 + LANES)
            xcat = p_ext[:, c1 - c0:c1 - c0 + LANES]
            acc = jnp.zeros((tm, LANES), F32) + cb_ref[:, sl]
            for k in range(SSM_CONV):
                sh = (SSM_CONV // 2 - k) % rows
                rolled = xcat if sh == 0 else pltpu.roll(xcat, sh, axis=0)
                acc = acc + rolled[CONV_HALO:CONV_HALO + tm] * cw_ref[k:k + 1, sl]
            o_xbc[:, sl] = _silu(acc).astype(o_xbc.dtype)

    project(w_z, o_z)
    project(w_dt, o_dt)
    if with_gate:
        project(w_gate, refs.pop(0))


def _in_proj(x2, g, sc, sh, seq_len, per_seq_mod, w, lw, tables, with_gate, tm):
    m, d = x2.shape
    rotate = tables is not None
    seq_tiles = seq_len // tm
    mod_map = (lambda i: (i // seq_tiles, 0, 0)) if per_seq_mod else (lambda i: (0, 0, 0))
    weights = [w['na'], w['mla'], w['gqa'], w['xbc'], w['z'], w['dt']] + ([w['gate']] if with_gate else [])
    bd = jnp.asarray(np.kron(np.eye(2), np.ones((HALF, HALF))), BF16)
    nch = w['xbc'].shape[1]
    conv_w = jnp.zeros((8, nch), F32).at[:SSM_CONV].set(lw['ssm_conv_w'])
    params = [lw['mla_g_q'].reshape(1, -1), lw['mla_g_kv'].reshape(1, -1), w['mla_uq'], w['mla_uk'], w['mla_uv'],
              jnp.tile(lw['gqa_g_q'], 2).reshape(1, LANES), jnp.tile(lw['gqa_g_k'], 2).reshape(1, LANES), bd,
              conv_w, lw['ssm_conv_b'].reshape(1, nch)]
    hb = tm // CONV_HALO
    last = m // CONV_HALO - 1
    in_specs = [pl.BlockSpec((tm, d), lambda i: (i, 0)),
                pl.BlockSpec((CONV_HALO, d), lambda i: (jnp.maximum(i * hb - 1, 0), 0)),
                pl.BlockSpec((CONV_HALO, d), lambda i: (jnp.minimum((i + 1) * hb, last), 0)),
                _resident((1, d)), pl.BlockSpec((1, 1, d), mod_map), pl.BlockSpec((1, 1, d), mod_map)]
    in_specs += [_resident(a.shape) for a in weights + params]
    args = [x2, x2, x2, g.reshape(1, d), sc, sh] + weights + params
    if rotate:
        in_specs += [pl.BlockSpec((tm, LANES), lambda i: (i % seq_tiles, 0))] * 4
        args += list(tables['mla']) + list(tables['gqa'])
    outs = [(w['na'].shape[1], BF16),
            (MLA_HEADS * LANES, BF16), (MLA_HEADS * LANES, BF16), (MLA_HEADS * MLA_V, BF16),
            (GQA_Q_HEADS * GQA_HEAD_DIM, BF16), (GQA_KV_HEADS * QUAD_W, BF16), (GQA_KV_HEADS * QUAD_W, BF16),
            (nch, BF16), (w['z'].shape[1], BF16), (w['dt'].shape[1], F32)]
    if with_gate:
        outs.append((w['gate'].shape[1], BF16))
    return pl.pallas_call(
        functools.partial(_in_proj_kernel, with_gate=with_gate, rotate=rotate, seq_tiles=seq_tiles),
        grid=(m // tm,),
        in_specs=in_specs,
        out_specs=[pl.BlockSpec((tm, n), lambda i: (i, 0)) for n, _ in outs],
        out_shape=[jax.ShapeDtypeStruct((m, n), dt) for n, dt in outs],
        compiler_params=_params(1),
        name="in_proj",
    )(*args)


def _na_kernel(q_ref, k_ref, v_ref, kc_ref, vc_ref, bias0_ref, bias1_ref, bias2_ref, o_ref):
    bias_refs = (bias0_ref, bias1_ref, bias2_ref)
    lo = _lane() < HALF
    kc = kc_ref[...]
    vc = vc_ref[...]
    qscale = NA_HEAD_DIM ** -0.5 * LOG2E
    nq = NA_BLK * GRID_W
    nk = NA_WIN_ROWS * GRID_W
    for bi in range(NA_BLOCKS):
        r0, ws = bi * NA_BLK, _na_window_start(bi)
        q = q_ref[r0 * GRID_W:r0 * GRID_W + nq, :]
        q = (q.astype(F32) * qscale).astype(q.dtype)
        kw = k_ref[ws * GRID_W:ws * GRID_W + nk, :]
        vw = v_ref[ws * GRID_W:ws * GRID_W + nk, :]
        zero = jnp.zeros_like(q)
        q2 = jnp.concatenate([jnp.where(lo, q, zero), jnp.where(lo, zero, q)], axis=0)
        s_loc = _dot_nt(q2, kw) + bias_refs[_na_block_kind(bi)][0]
        s_ctx = _dot_nt(q2, kc)
        m = jnp.maximum(jnp.max(s_loc, axis=-1, keepdims=True), jnp.max(s_ctx, axis=-1, keepdims=True))
        p_loc = jnp.exp2(s_loc - m)
        p_ctx = jnp.exp2(s_ctx - m)
        den = jnp.sum(p_loc, axis=-1, keepdims=True) + jnp.sum(p_ctx, axis=-1, keepdims=True)
        o = (_dot(p_loc.astype(BF16), vw) + _dot(p_ctx.astype(BF16), vc)) / den
        o_ref[r0 * GRID_W:r0 * GRID_W + nq, :] = jnp.where(lo, o[:nq], o[nq:]).astype(o_ref.dtype)


def _na_window_start(bi):
    return min(max(bi * NA_BLK - NA_ROW_WIN // 2, 0), GRID_ROWS - NA_WIN_ROWS)


def _na_block_kind(bi):
    return 0 if bi == 0 else (2 if bi == NA_BLOCKS - 1 else 1)


def _na_bias_table(rpb_all):
    n_heads = rpb_all.shape[0] * NA_HEADS
    n_rho = 2 * NA_ROW_WIN - 1
    n_off = 2 * NA_COL_WIN - 1
    w = np.arange(GRID_W)
    j = np.arange(GRID_W)
    cs = np.clip(w - NA_COL_WIN // 2, 0, GRID_W - NA_COL_WIN)
    valid = (j[None, :] >= cs[:, None]) & (j[None, :] < cs[:, None] + NA_COL_WIN)
    r = rpb_all.astype(F32).reshape(n_heads, n_rho, n_off) * LOG2E
    r = jnp.pad(r, ((0, 0), (0, 0), (0, GRID_W)))
    period = n_off + GRID_W
    skew = jnp.broadcast_to(r[:, :, None, :], (n_heads, n_rho, GRID_W, period)).reshape(n_heads, n_rho, GRID_W * period)
    skew = skew[:, :, :GRID_W * (period - 1)].reshape(n_heads, n_rho, GRID_W, period - 1)
    t = jnp.where(valid[None, None], skew[:, :, :, NA_COL_WIN - 1:NA_COL_WIN - 1 + GRID_W], NEG)
    tt = t.transpose(0, 2, 1, 3).reshape(n_heads, GRID_W, n_rho * GRID_W)
    ttp = jnp.pad(tt, ((0, 0), (0, 0), (NA_BLK * GRID_W, NA_BLK * GRID_W)), constant_values=NEG)
    nk = NA_WIN_ROWS * GRID_W
    patterns = {}
    for bi in range(NA_BLOCKS):
        ws = _na_window_start(bi)
        pat = []
        for u in range(NA_BLK):
            r = bi * NA_BLK + u
            rs = min(max(r - NA_ROW_WIN // 2, 0), GRID_ROWS - NA_ROW_WIN)
            rho0 = ws - r + NA_ROW_WIN - 1
            in_win = tuple(0 <= ws + i - rs < NA_ROW_WIN for i in range(NA_WIN_ROWS))
            pat.append((rho0, in_win))
        assert patterns.setdefault(_na_block_kind(bi), pat) == pat
    kinds = []
    for kind in range(3):
        per_u = []
        for rho0, in_win in patterns[kind]:
            start = (rho0 + NA_BLK) * GRID_W
            mask = np.repeat(np.asarray(in_win), GRID_W)
            per_u.append(jnp.where(mask[None, None, :], ttp[:, :, start:start + nk], NEG))
        kinds.append(jnp.stack(per_u, axis=1).reshape(n_heads // 2, 2 * NA_BLK * GRID_W, nk))
    return kinds


def _na_attention(o_na_l, o_na_c, bias_kinds, layer, nb):
    bias_spec = pl.BlockSpec((1, 2 * NA_BLK * GRID_W, NA_WIN_ROWS * GRID_W), lambda b, p: (layer * N_PAIRS + p, 0, 0))
    return pl.pallas_call(
        _na_kernel,
        grid=(nb, N_PAIRS),
        in_specs=[pl.BlockSpec((SEQ, LANES), lambda b, p: (b, p)),
                  pl.BlockSpec((SEQ, LANES), lambda b, p: (b, N_PAIRS + p)),
                  pl.BlockSpec((SEQ, LANES), lambda b, p: (b, 2 * N_PAIRS + p)),
                  pl.BlockSpec((CTX_LEN, LANES), lambda b, p: (b, N_PAIRS + p)),
                  pl.BlockSpec((CTX_LEN, LANES), lambda b, p: (b, 2 * N_PAIRS + p)),
                  bias_spec, bias_spec, bias_spec],
        out_specs=pl.BlockSpec((SEQ, LANES), lambda b, p: (b, p)),
        out_shape=jax.ShapeDtypeStruct((nb * SEQ, NA_HEADS * NA_HEAD_DIM), BF16),
        compiler_params=_params(2),
        name="na_attn",
    )(o_na_l, o_na_l, o_na_l, o_na_c, o_na_c, *bias_kinds)


def _attn_kernel(*refs, n_src, dq, scale):
    q_ref = refs[0]
    o_ref = refs[-1]
    lo = _lane() < HALF
    lane_o = _lane((1, QUAD_W))
    for quad in range(ATTN_HEADS // QUAD):
        vsl = slice(quad * QUAD_W, (quad + 1) * QUAD_W)
        outs = []
        for a in range(quad * QUAD, (quad + 1) * QUAD):
            c0 = a * LANES if dq == LANES else (a // 2) * LANES
            qa = q_ref[:, c0:c0 + LANES]
            if scale != 1.0:
                qa = (qa.astype(F32) * scale).astype(qa.dtype)
            if dq != LANES:
                zero = jnp.zeros_like(qa)
                qa = jnp.where(lo, qa, zero) if a % 2 == 0 else jnp.where(lo, zero, qa)
            ss = [_dot_nt(qa, refs[1 + 2 * s][:, c0:c0 + LANES]) for s in range(n_src)]
            m = ss[0].max(axis=-1, keepdims=True)
            for sc in ss[1:]:
                m = jnp.maximum(m, sc.max(axis=-1, keepdims=True))
            den = None
            o = None
            for s in range(n_src):
                p = jnp.exp2(ss[s] - m)
                ps = jnp.sum(p, axis=-1, keepdims=True)
                po = _dot(p.astype(BF16), refs[2 + 2 * s][:, vsl])
                den = ps if den is None else den + ps
                o = po if o is None else o + po
            outs.append(o / den)
        out = outs[QUAD - 1]
        for a in range(QUAD - 2, -1, -1):
            out = jnp.where(lane_o < (a + 1) * HALF, outs[a], out)
        o_ref[:, vsl] = out.astype(o_ref.dtype)


def _attention(q, srcs, *, nb, t, dq, tq, scale=1.0):
    nq = t // tq
    in_specs = [pl.BlockSpec((tq, ATTN_HEADS * dq), lambda b, i: (b * nq + i, 0))]
    args = [q]
    for k_arr, v_arr, u, k_off, v_off in srcs:
        in_specs.append(pl.BlockSpec((u, ATTN_HEADS * dq), lambda b, i, k_off=k_off: (b, k_off)))
        in_specs.append(pl.BlockSpec((u, ATTN_HEADS * HALF), lambda b, i, v_off=v_off: (b, v_off)))
        args += [k_arr, v_arr]
    return pl.pallas_call(
        functools.partial(_attn_kernel, n_src=len(srcs), dq=dq, scale=scale),
        grid=(nb, nq),
        in_specs=in_specs,
        out_specs=pl.BlockSpec((tq, ATTN_HEADS * HALF), lambda b, i: (b * nq + i, 0)),
        out_shape=jax.ShapeDtypeStruct((nb * t, ATTN_HEADS * HALF), BF16),
        compiler_params=_params(2),
        name="attn",
    )(*args)


def _mla_prep(x, gq_ref, gkv_ref, wq_ref, wk_ref, wv_ref, cos_ref, sin_ref, q_ref, k_ref, v_ref):
    rotate = cos_ref is not None
    scale = (MLA_NOPE + MLA_ROPE) ** -0.5 * LOG2E
    cq = x[:, :MLA_Q_RANK]
    ckv = x[:, MLA_Q_RANK:MLA_Q_RANK + MLA_KV_RANK]
    kr = x[:, MLA_Q_RANK + MLA_KV_RANK:]
    cqn = _rms(cq, gq_ref[...]).astype(BF16)
    ckvn = _rms(ckv, gkv_ref[...]).astype(BF16)
    lane = _lane()
    first_half = lane < MLA_NOPE + MLA_ROPE // 2

    def rope(t):
        if not rotate:
            return t
        rot = jnp.where(first_half, pltpu.roll(t, LANES - MLA_ROPE // 2, axis=1), pltpu.roll(t, MLA_ROPE // 2, axis=1))
        return t * cos_ref[...] + rot * sin_ref[...]

    kr = rope(kr)
    for h in range(MLA_HEADS):
        sl = slice(h * LANES, (h + 1) * LANES)
        qh = rope(_dot(cqn, wq_ref[:, sl]))
        q_ref[:, sl] = (qh * scale).astype(q_ref.dtype)
        k_ref[:, sl] = (_dot(ckvn, wk_ref[:, sl]) + kr).astype(k_ref.dtype)
    v_ref[...] = _dot(ckvn, wv_ref[...]).astype(v_ref.dtype)


def _gqa_prep(x, gq_ref, gk_ref, bd_ref, cos_ref, sin_ref, q_ref, k_ref, v_ref):
    rotate = cos_ref is not None
    scale = GQA_HEAD_DIM ** -0.5 * LOG2E
    lane = _lane()
    lo = lane < HALF
    first_half = (lane & (GQA_HEAD_DIM - 1)) < GQA_HEAD_DIM // 2

    def headnorm(t, g):
        ss = _dot((t * t).astype(BF16), bd_ref[...]) * (1.0 / GQA_HEAD_DIM)
        return t * lax.rsqrt(ss + EPS) * g

    def rope(t):
        if not rotate:
            return t
        rot = jnp.where(first_half, pltpu.roll(t, LANES - GQA_HEAD_DIM // 2, axis=1), pltpu.roll(t, GQA_HEAD_DIM // 2, axis=1))
        return t * cos_ref[...] + rot * sin_ref[...]

    for j in range(N_PAIRS):
        sl = slice(j * LANES, (j + 1) * LANES)
        q_ref[:, sl] = (rope(headnorm(x[:, sl], gq_ref[...])) * scale).astype(q_ref.dtype)
    nq = GQA_Q_HEADS * GQA_HEAD_DIM
    k = rope(headnorm(x[:, nq:nq + LANES], gk_ref[...]))
    v = x[:, nq + LANES:nq + 2 * LANES]
    for t, ref in ((k, k_ref), (v, v_ref)):
        sw = pltpu.roll(t, HALF, axis=1)
        for g, dup in enumerate((jnp.where(lo, t, sw), jnp.where(lo, sw, t))):
            dup = dup.astype(ref.dtype)
            ref[:, (2 * g) * LANES:(2 * g + 1) * LANES] = dup
            ref[:, (2 * g + 1) * LANES:(2 * g + 2) * LANES] = dup


def _rope_tables(dim, lane_start, group):
    t = jnp.arange(SEQ)
    row = (t // GRID_W).astype(F32)
    col = (t % GRID_W).astype(F32)
    quarter = dim // 4
    inv = ROPE_THETA ** (-jnp.arange(quarter, dtype=F32) / quarter)
    ang = jnp.concatenate([row[:, None] * inv, col[:, None] * inv], axis=-1)
    cos, sin = jnp.cos(ang), jnp.sin(ang)
    cos_g = jnp.ones((SEQ, group), F32).at[:, lane_start:lane_start + dim].set(jnp.concatenate([cos, cos], axis=-1))
    sin_g = jnp.zeros((SEQ, group), F32).at[:, lane_start:lane_start + dim].set(jnp.concatenate([-sin, sin], axis=-1))
    reps = LANES // group
    return jnp.tile(cos_g, (1, reps)), jnp.tile(sin_g, (1, reps))


def _softplus(x):
    return jnp.maximum(x, 0.0) + jnp.log(1.0 + jnp.exp(-jnp.abs(x)))


def _ssd_kernel(*refs, with_y):
    n_dir = 4 if with_y else 3
    dir_refs = (refs[:n_dir], refs[n_dir:2 * n_dir])
    al_ref, bi_ref, h0_ref = refs[2 * n_dir:2 * n_dir + 3]
    if with_y:
        yf_ref, yb_ref, hf_ref, st = refs[2 * n_dir + 3:]
        y_refs = (yf_ref, yb_ref)
    else:
        hf_ref, st = refs[2 * n_dir + 3:]
    j = pl.program_id(1)
    q = SSM_CHUNK

    @pl.when(j == 0)
    def _():
        st[...] = h0_ref[:, 0]

    row = lax.broadcasted_iota(jnp.int32, (q, q), 0)
    col = lax.broadcasted_iota(jnp.int32, (q, q), 1)
    lane = _lane()
    lo = lane < HALF

    def col_of(x, h):
        return jnp.sum(jnp.where(lane == h, x, 0.0), axis=1, keepdims=True)

    decay = []
    for d in range(2):
        dt_ref = dir_refs[d][-1]
        m_qs = (col <= row) if d == 0 else (col >= row)
        dt = _softplus(dt_ref[...] + bi_ref[d])
        da = jnp.where(lane < SSM_HEADS, dt * (-jnp.exp(al_ref[d])), 0.0)
        a_cum_c = jnp.dot(jnp.where(m_qs, 1.0, 0.0), da, precision=HIGHEST, preferred_element_type=F32)
        a_tot = jnp.sum(da, axis=0, keepdims=True)
        decay.append((m_qs, dt, a_cum_c, a_tot, a_cum_c.T))

    for d in range(2):
        if with_y:
            x_ref, b_ref, c_ref, dt_ref = dir_refs[d]
        else:
            x_ref, b_ref, dt_ref = dir_refs[d]
        m_qs, dt, a_cum_c, a_tot, a_cum_r = decay[d]

        x = x_ref[...].astype(F32)
        if with_y:
            cm = c_ref[...]
        for g in range(SSM_GROUPS):
            btg = b_ref[:, g * SSM_STATE:(g + 1) * SSM_STATE].astype(F32).T.astype(BF16)
            if with_y:
                cg = cm[:, g * SSM_STATE:(g + 1) * SSM_STATE]
                cb = _dot(cg, btg)
            for pp in range(2):
                p = 2 * g + pp
                h0, h1 = 2 * p, 2 * p + 1
                ac0, ac1 = col_of(a_cum_c, h0), col_of(a_cum_c, h1)
                tot0, tot1 = col_of(a_tot, h0), col_of(a_tot, h1)
                xdt = x[:, p * LANES:(p + 1) * LANES] * jnp.where(lo, col_of(dt, h0), col_of(dt, h1))
                ht = st[d, p]
                if with_y:
                    xdt_b = xdt.astype(BF16)
                    l0 = jnp.exp(jnp.where(m_qs, ac0 - a_cum_r[h0:h0 + 1, :], NEG))
                    l1 = jnp.exp(jnp.where(m_qs, ac1 - a_cum_r[h1:h1 + 1, :], NEG))
                    y_in = jnp.where(lo, _dot((cb * l0).astype(BF16), xdt_b), _dot((cb * l1).astype(BF16), xdt_b))
                    y_st = _dot(cg, ht.astype(BF16)) * jnp.where(lo, jnp.exp(ac0), jnp.exp(ac1))
                    y_refs[d][:, p * LANES:(p + 1) * LANES] = (y_in + y_st).astype(y_refs[d].dtype)
                dec = jnp.where(lo, jnp.exp(tot0 - ac0), jnp.exp(tot1 - ac1))
                st[d, p] = ht * jnp.where(lo, jnp.exp(tot0), jnp.exp(tot1)) + _dot(btg, (xdt * dec).astype(BF16))

    @pl.when(j == pl.num_programs(1) - 1)
    def _():
        hf_ref[:, 0] = st[...]


def _ssd(xbc, o_dt, a_log, dt_bias, h0, *, nb, t, with_y):
    nc = t // SSM_CHUNK
    pad = LANES - SSM_HEADS
    al_row = jnp.pad(a_log.astype(F32), ((0, 0), (0, pad))).reshape(2, 1, LANES)
    bi_row = jnp.pad(dt_bias.astype(F32), ((0, 0), (0, pad))).reshape(2, 1, LANES)
    x_blk = SSM_INNER // SSM_BC
    in_specs, args = [], []
    for d in range(2):
        cidx = (lambda j: j) if d == 0 else (lambda j: nc - 1 - j)
        in_specs += [pl.BlockSpec((SSM_CHUNK, SSM_INNER), lambda b, j, c=cidx: (b * nc + c(j), 0)),
                     pl.BlockSpec((SSM_CHUNK, SSM_BC), lambda b, j, c=cidx: (b * nc + c(j), x_blk))]
        args += [xbc, xbc]
        if with_y:
            in_specs.append(pl.BlockSpec((SSM_CHUNK, SSM_BC), lambda b, j, c=cidx: (b * nc + c(j), x_blk + 1)))
            args.append(xbc)
        in_specs.append(pl.BlockSpec((SSM_CHUNK, LANES), lambda b, j, c=cidx, d=d: (b * nc + c(j), d)))
        args.append(o_dt)
    st_spec = pl.BlockSpec((2, 1, N_PAIRS, SSM_STATE, LANES), lambda b, j: (0, b, 0, 0, 0))
    in_specs += [_resident(al_row.shape), _resident(bi_row.shape), st_spec]
    args += [al_row, bi_row, h0]
    st_shape = jax.ShapeDtypeStruct((2, nb, N_PAIRS, SSM_STATE, LANES), F32)
    out_specs, out_shape = [st_spec], [st_shape]
    if with_y:
        out_specs = [pl.BlockSpec((SSM_CHUNK, SSM_INNER), lambda b, j: (b * nc + j, 0)),
                     pl.BlockSpec((SSM_CHUNK, SSM_INNER), lambda b, j: (b * nc + nc - 1 - j, 0))] + out_specs
        out_shape = [jax.ShapeDtypeStruct((nb * t, SSM_INNER), BF16)] * 2 + out_shape
    return pl.pallas_call(
        functools.partial(_ssd_kernel, with_y=with_y),
        grid=(nb, nc),
        in_specs=in_specs, out_specs=out_specs, out_shape=out_shape,
        scratch_shapes=[pltpu.VMEM((2, N_PAIRS, SSM_STATE, LANES), F32)],
        compiler_params=_params(2),
        name="ssd",
    )(*args)


def _merge_kernel(yna_ref, ymla_ref, ygqa_ref, yf_ref, yb_ref, xs_ref, z_ref, dsk_ref, gssm_ref,
                  gate_ref, wb_ref, wo_ref, x_ref, gt_ref, g_ref, o_ref):
    y = yf_ref[...].astype(F32) + yb_ref[...].astype(F32) + dsk_ref[...] * xs_ref[...].astype(F32)
    y_ssm = _rms(y * _silu(z_ref[...].astype(F32)), gssm_ref[...]).astype(BF16)
    acc = None
    for k, yk in enumerate((yna_ref[...], ymla_ref[...], ygqa_ref[...], y_ssm)):
        proj = _dot(yk, wb_ref[k])
        gate = jax.nn.sigmoid(gate_ref[:, k * D_MODEL:(k + 1) * D_MODEL].astype(F32))
        acc = gate * proj if acc is None else acc + gate * proj
    mix = _dot(acc.astype(BF16), wo_ref[...])
    o_ref[...] = x_ref[...] + gt_ref[0] * _rms(mix, g_ref[...])


def _merge(ys, y_f, y_b, xbc, z, d_skip, g_norm, gate, wb, wo, x2, gt, g_post, tiles_per_mod, tm=MERGE_TM):
    m, d = x2.shape
    mod_map = (lambda i: (0, 0, 0)) if tiles_per_mod is None else (lambda i: (i // tiles_per_mod, 0, 0))
    dvec = jnp.repeat(d_skip.astype(F32), SSM_HEAD_DIM).reshape(1, SSM_INNER)
    row_blk = pl.BlockSpec((tm, BRANCH_W), lambda i: (i, 0))
    in_specs = [row_blk] * (N_BRANCH - 1) + [row_blk, row_blk, row_blk, row_blk,
                                             _resident((1, SSM_INNER)), _resident((1, SSM_INNER))]
    in_specs += [pl.BlockSpec((tm, N_BRANCH * d), lambda i: (i, 0)),
                 _resident(wb.shape), _resident(wo.shape),
                 pl.BlockSpec((tm, d), lambda i: (i, 0)),
                 pl.BlockSpec((1, 1, d), mod_map), _resident((1, d))]
    return pl.pallas_call(
        _merge_kernel,
        grid=(m // tm,),
        in_specs=in_specs,
        out_specs=pl.BlockSpec((tm, d), lambda i: (i, 0)),
        out_shape=jax.ShapeDtypeStruct((m, d), F32),
        compiler_params=_params(1),
        name="merge",
    )(*ys, y_f, y_b, xbc, z, dvec, g_norm.reshape(1, SSM_INNER), gate, wb, wo, x2, gt, g_post.reshape(1, d))


def _ffn_kernel(x_ref, gpre_ref, sc_ref, sh_ref, w1_ref, w3_ref, w2_ref, gt_ref, gpost_ref, o_ref):
    x = x_ref[...]
    hb = (_rms(x, gpre_ref[...]) * (1.0 + sc_ref[0]) + sh_ref[0]).astype(BF16)
    u = _silu(_dot(hb, w1_ref[...])) * _dot(hb, w3_ref[...])
    f = _dot(u.astype(BF16), w2_ref[...])
    o_ref[...] = x + gt_ref[0] * _rms(f, gpost_ref[...])


def _ffn(x2, g_pre, sc, sh, w1, w3, w2, gt, g_post, tiles_per_mod, tm=MERGE_TM):
    m, d = x2.shape
    mod_map = (lambda i: (0, 0, 0)) if tiles_per_mod is None else (lambda i: (i // tiles_per_mod, 0, 0))
    mod = pl.BlockSpec((1, 1, d), mod_map)
    return pl.pallas_call(
        _ffn_kernel,
        grid=(m // tm,),
        in_specs=[pl.BlockSpec((tm, d), lambda i: (i, 0)), _resident((1, d)), mod, mod,
                  _resident(w1.shape), _resident(w3.shape), _resident(w2.shape), mod, _resident((1, d))],
        out_specs=pl.BlockSpec((tm, d), lambda i: (i, 0)),
        out_shape=jax.ShapeDtypeStruct((m, d), F32),
        compiler_params=_params(1),
        name="ffn",
    )(x2, g_pre.reshape(1, d), sc, sh, w1, w3, w2, gt, g_post.reshape(1, d))


def _split_w_in(w_in):
    off, cols = 0, {}
    for name, w in KEY_SPLITS + QUERY_SPLITS:
        cols[name] = w_in[:, off:off + w]
        off += w
    d = w_in.shape[0]
    z = lambda n: jnp.zeros((d, n), w_in.dtype)
    cat = lambda parts: jnp.concatenate(parts, axis=1).astype(BF16)
    w_na = cat([cols['na_q'], cols['na_k'], cols['na_v']])
    w_mla = cat([cols['mla_cq'], cols['mla_ckv'], z(MLA_NOPE), cols['mla_kr'], z(LANES - MLA_NOPE - MLA_ROPE)])
    w_gqa = cat([cols['gqa_q'], cols['gqa_k'], cols['gqa_v']])
    w_xbc = cat([cols['ssm_x'], cols['ssm_B'], cols['ssm_C']])
    w_z = cols['ssm_z'].astype(BF16)
    w_dt = cat([cols['ssm_dt'][:, :SSM_HEADS], z(LANES - SSM_HEADS), cols['ssm_dt'][:, SSM_HEADS:], z(LANES - SSM_HEADS)])
    w_gate = w_in[:, MIX_COLS:].astype(BF16)
    return dict(na=w_na, mla=w_mla, gqa=w_gqa, xbc=w_xbc, z=w_z, dt=w_dt, gate=w_gate)


def _layer(x2, xc2, mods, mc, lw, tables, layer, nb, need_ctx):
    d = D_MODEL
    tpm = SEQ // MERGE_TM
    chunk = lambda a, k: a[:, k * d:(k + 1) * d].reshape(-1, 1, d)
    sh1, sc1, gt1, sh2, sc2, gt2 = (chunk(mods, k) for k in range(6))
    csh1, csc1, cgt1, csh2, csc2, cgt2 = (chunk(mc, k) for k in range(6))
    w = _split_w_in(lw['w_in'])
    w['mla_uq'] = jnp.pad(lw['mla_w_uq'].reshape(MLA_Q_RANK, MLA_HEADS, MLA_NOPE + MLA_ROPE),
                          ((0, 0), (0, 0), (0, LANES - MLA_NOPE - MLA_ROPE))).reshape(MLA_Q_RANK, MLA_HEADS * LANES)
    wkv = lw['mla_w_ukv'].reshape(MLA_KV_RANK, MLA_HEADS, MLA_NOPE + MLA_V)
    w['mla_uk'] = jnp.pad(wkv[:, :, :MLA_NOPE], ((0, 0), (0, 0), (0, LANES - MLA_NOPE))).reshape(MLA_KV_RANK, MLA_HEADS * LANES)
    w['mla_uv'] = wkv[:, :, MLA_NOPE:].reshape(MLA_KV_RANK, MLA_HEADS * MLA_V)
    o_na, ql, kl, vl, gql, gkl, gvl, xbc_l, z_l, o_dt, gate = _in_proj(
        x2, lw['g_pre1'], sc1, sh1, SEQ, True, w, lw, tables, True, IN_PROJ_TM)
    res = _in_proj(xc2, lw['g_pre1'], csc1, csh1, CTX_LEN, False, w, lw, None, need_ctx, CTX_LEN)
    c_na, qc, kc, vc, gqc, gkc, gvc, xbc_c, z_c, c_dt = res[:10]
    c_gate = res[10] if need_ctx else None

    y_na = _na_attention(o_na, c_na, tables['na_bias'], layer, nb)
    y_mla = _attention(ql, [(kc, vc, CTX_LEN, 0, 0), (kl, vl, SEQ, 0, 0)], nb=nb, t=SEQ, dq=LANES, tq=256)
    y_gqa = _attention(gql, [(gkc, gvc, CTX_LEN, 0, 0), (gkl, gvl, SEQ, 0, 0)], nb=nb, t=SEQ, dq=HALF, tq=256)

    h_zero = jnp.zeros((2, nb, N_PAIRS, SSM_STATE, LANES), F32)
    res_c = _ssd(xbc_c, c_dt, lw['ssm_a_log'], lw['ssm_dt_bias'], h_zero, nb=nb, t=CTX_LEN, with_y=need_ctx)
    h_ctx = res_c[-1]
    yf_l, yb_l, _ = _ssd(xbc_l, o_dt, lw['ssm_a_log'], lw['ssm_dt_bias'], h_ctx, nb=nb, t=SEQ, with_y=True)

    wb = lw['w_branch'].astype(BF16)
    wo = lw['w_out'].astype(BF16)
    w1, w3, w2 = lw['ffn_w1'].astype(BF16), lw['ffn_w3'].astype(BF16), lw['ffn_w2'].astype(BF16)
    x2 = _merge((y_na, y_mla, y_gqa), yf_l, yb_l, xbc_l, z_l, lw['ssm_d'], lw['ssm_g_norm'],
                gate, wb, wo, x2, gt1, lw['g_post1'], tpm)
    x2 = _ffn(x2, lw['g_pre2'], sc2, sh2, w1, w3, w2, gt2, lw['g_post2'], tpm)

    if need_ctx:
        yc_na = _attention(c_na, [(c_na, c_na, CTX_LEN, 1, 2)], nb=nb, t=CTX_LEN, dq=HALF,
                           tq=CTX_LEN, scale=NA_HEAD_DIM ** -0.5 * LOG2E)
        yc_mla = _attention(qc, [(kc, vc, CTX_LEN, 0, 0)], nb=nb, t=CTX_LEN, dq=LANES, tq=CTX_LEN)
        yc_gqa = _attention(gqc, [(gkc, gvc, CTX_LEN, 0, 0)], nb=nb, t=CTX_LEN, dq=HALF, tq=CTX_LEN)
        xc2 = _merge((yc_na, yc_mla, yc_gqa), res_c[0], res_c[1], xbc_c, z_c, lw['ssm_d'], lw['ssm_g_norm'],
                     c_gate, wb, wo, xc2, cgt1, lw['g_post1'], None)
        xc2 = _ffn(xc2, lw['g_pre2'], csc2, csh2, w1, w3, w2, cgt2, lw['g_post2'], None)
    return x2, xc2


def kernel(x, c, ctx, c_ctx, w_ada, b_ada, g_pre1, g_post1, g_pre2, g_post2, w_in, na_rpb, mla_g_q, mla_g_kv, mla_w_uq, mla_w_ukv, gqa_g_q, gqa_g_k, ssm_conv_w, ssm_conv_b, ssm_a_log, ssm_dt_bias, ssm_d, ssm_g_norm, w_branch, w_out, ffn_w1, ffn_w3, ffn_w2):
    nb, s, d = x.shape
    assert s == SEQ and d == D_MODEL and ctx.shape[1] == CTX_LEN
    depth = w_ada.shape[0]
    tables = {'mla': _rope_tables(MLA_ROPE, MLA_NOPE, LANES), 'gqa': _rope_tables(GQA_HEAD_DIM, 0, GQA_HEAD_DIM),
              'na_bias': _na_bias_table(na_rpb)}
    x2 = x.reshape(nb * s, d)
    xc2 = ctx.reshape(nb * CTX_LEN, d)
    n_rows = -(-(nb + 1) // 8) * 8
    c_rows = jnp.zeros((n_rows, d), F32).at[:nb].set(c).at[nb].set(c_ctx)
    stacked = dict(w_in=w_in, na_rpb=na_rpb, mla_g_q=mla_g_q, mla_g_kv=mla_g_kv, mla_w_uq=mla_w_uq,
                   mla_w_ukv=mla_w_ukv, gqa_g_q=gqa_g_q, gqa_g_k=gqa_g_k, ssm_conv_w=ssm_conv_w,
                   ssm_conv_b=ssm_conv_b, ssm_a_log=ssm_a_log, ssm_dt_bias=ssm_dt_bias, ssm_d=ssm_d,
                   ssm_g_norm=ssm_g_norm, w_branch=w_branch, w_out=w_out, ffn_w1=ffn_w1, ffn_w3=ffn_w3,
                   ffn_w2=ffn_w2, g_pre1=g_pre1, g_post1=g_post1, g_pre2=g_pre2, g_post2=g_post2)
    for k in ('w_in', 'mla_w_uq', 'mla_w_ukv', 'w_branch', 'w_out', 'ffn_w1', 'ffn_w3', 'ffn_w2'):
        stacked[k] = stacked[k].astype(BF16)
    for l in range(depth):
        lw = {k: v[l] for k, v in stacked.items()}
        mod_all = _ada(c_rows, w_ada[l], b_ada[l])
        x2, xc2 = _layer(x2, xc2, mod_all[:nb], mod_all[nb:nb + 1], lw, tables, l, nb, l < depth - 1)
    return x2.reshape(nb, s, d)
```

```python
import functools

import numpy as np
import jax
import jax.numpy as jnp
from jax import lax
from jax.experimental import pallas as pl
from jax.experimental.pallas import tpu as pltpu

F32 = jnp.float32
BF16 = jnp.bfloat16
HIGHEST = lax.Precision.HIGHEST

D_MODEL = 1024
SEQ = 2048
GRID_W = 64
GRID_ROWS = SEQ // GRID_W
CTX_LEN = 256
N_BRANCH = 4
BRANCH_W = 512
ROPE_THETA = 10000.0
EPS = 1e-6
NA_HEADS = 8
NA_HEAD_DIM = 64
NA_ROW_WIN = 8
NA_COL_WIN = 16
MLA_HEADS = 8
MLA_Q_RANK = 384
MLA_KV_RANK = 256
MLA_NOPE = 64
MLA_ROPE = 32
MLA_V = 64
GQA_Q_HEADS = 8
GQA_KV_HEADS = 2
GQA_HEAD_DIM = 64
SSM_HEADS = 8
SSM_HEAD_DIM = 64
SSM_GROUPS = 2
SSM_STATE = 128
SSM_CONV = 5
SSM_CHUNK = 128
SSM_INNER = SSM_HEADS * SSM_HEAD_DIM
SSM_BC = SSM_GROUPS * SSM_STATE
FFN_HIDDEN = ((8 * D_MODEL + 3 * 256 - 1) // (3 * 256)) * 256

KEY_SPLITS = (('na_k', 512), ('na_v', 512), ('mla_ckv', MLA_KV_RANK), ('mla_kr', MLA_ROPE),
              ('gqa_k', 128), ('gqa_v', 128), ('ssm_x', SSM_INNER), ('ssm_B', SSM_BC), ('ssm_dt', 2 * SSM_HEADS))
QUERY_SPLITS = (('na_q', 512), ('mla_cq', MLA_Q_RANK), ('gqa_q', 512), ('ssm_C', SSM_BC), ('ssm_z', SSM_INNER))
MIX_COLS = sum(w for _, w in KEY_SPLITS + QUERY_SPLITS)

LANES = 128
HALF = 64
N_PAIRS = 4
NEG = -1e30
LOG2E = 1.4426950408889634
NA_BLK = 4
NA_BLOCKS = GRID_ROWS // NA_BLK
NA_WIN_ROWS = 12
ATTN_HEADS = 8
QUAD = 4
QUAD_W = QUAD * HALF
VMEM_LIMIT = 58 * 1024 * 1024
IN_PROJ_TM = 512
MERGE_TM = 512
CONV_HALO = 16


def _params(n_axes):
    return pltpu.CompilerParams(dimension_semantics=("arbitrary",) * n_axes, vmem_limit_bytes=VMEM_LIMIT)


def _resident(shape):
    nd = len(shape)
    return pl.BlockSpec(shape, lambda *_: (0,) * nd, pipeline_mode=pl.Buffered(1))


def _dot(a, b):
    return jnp.dot(a, b, preferred_element_type=F32)


def _dot_nt(a, b):
    return lax.dot_general(a, b, (((1,), (1,)), ((), ())), preferred_element_type=F32)


def _silu(x):
    return x * jax.nn.sigmoid(x)


def _rms(x, g):
    return x * lax.rsqrt(jnp.mean(x * x, axis=-1, keepdims=True) + EPS) * g


def _lane(shape=(1, LANES)):
    return lax.broadcasted_iota(jnp.int32, shape, len(shape) - 1)


def _ada_kernel(c_ref, w_ref, b_ref, o_ref):
    a = _silu(c_ref[...]).astype(BF16)
    o_ref[...] = _dot(a, w_ref[...].astype(BF16)) + b_ref[...]


def _ada(c_rows, w_ada, b_ada):
    r, d = c_rows.shape
    n = w_ada.shape[1]
    tn = 1536
    return pl.pallas_call(
        _ada_kernel,
        grid=(n // tn,),
        in_specs=[pl.BlockSpec((r, d), lambda j: (0, 0)),
                  pl.BlockSpec((d, tn), lambda j: (0, j)),
                  pl.BlockSpec((1, tn), lambda j: (0, j))],
        out_specs=pl.BlockSpec((r, tn), lambda j: (0, j)),
        out_shape=jax.ShapeDtypeStruct((r, n), F32),
        compiler_params=_params(1),
        name="ada",
    )(c_rows, w_ada, b_ada.reshape(1, n))


def _in_proj_kernel(*refs, with_gate, rotate, seq_tiles):
    refs = list(refs)
    take = lambda n: [refs.pop(0) for _ in range(n)]
    x_ref, xp_ref, xn_ref, g_ref, sc_ref, sh_ref = take(6)
    w_na, w_mla, w_gqa, w_xbc, w_z, w_dt = take(6)
    w_gate = take(1)[0] if with_gate else None
    mla_params = take(5)
    gqa_params = take(3)
    cw_ref, cb_ref = take(2)
    mla_tabs = take(2) if rotate else (None, None)
    gqa_tabs = take(2) if rotate else (None, None)
    o_na, q_mla, k_mla, v_mla, q_gqa, k_gqa, v_gqa, o_xbc, o_z, o_dt = take(10)
    tm = x_ref.shape[0]

    def modulated(x):
        return (_rms(x, g_ref[...]) * (1.0 + sc_ref[0]) + sh_ref[0]).astype(BF16)

    hb = modulated(x_ref[...])

    def project(w_ref, o_ref):
        width = w_ref.shape[1]
        for c0 in range(0, width, 512):
            cw = min(512, width - c0)
            o_ref[:, c0:c0 + cw] = _dot(hb, w_ref[:, c0:c0 + cw]).astype(o_ref.dtype)

    project(w_na, o_na)
    _mla_prep(_dot(hb, w_mla[...]), *mla_params, *mla_tabs, q_mla, k_mla, v_mla)
    _gqa_prep(_dot(hb, w_gqa[...]), *gqa_params, *gqa_tabs, q_gqa, k_gqa, v_gqa)

    i = pl.program_id(0) % seq_tiles
    zero = jnp.zeros((CONV_HALO, x_ref.shape[1]), BF16)
    h_prev = jnp.where(i > 0, modulated(xp_ref[...]), zero)
    h_next = jnp.where(i < seq_tiles - 1, modulated(xn_ref[...]), zero)
    h_ext = jnp.concatenate([h_prev, hb, h_next], axis=0)
    rows = tm + 2 * CONV_HALO
    for c0 in range(0, w_xbc.shape[1], 2 * LANES):
        p_ext = _dot(h_ext, w_xbc[:, c0:c0 + 2 * LANES])
        for c1 in range(c0, c0 + 2 * LANES, LANES):
            sl = slice(c1, c1 + LANES)
            xcat = p_ext[:, c1 - c0:c1 - c0 + LANES]
            acc = jnp.zeros((tm, LANES), F32) + cb_ref[:, sl]
            for k in range(SSM_CONV):
                sh = (SSM_CONV // 2 - k) % rows
                rolled = xcat if sh == 0 else pltpu.roll(xcat, sh, axis=0)
                acc = acc + rolled[CONV_HALO:CONV_HALO + tm] * cw_ref[k:k + 1, sl]
            o_xbc[:, sl] = _silu(acc).astype(o_xbc.dtype)

    project(w_z, o_z)
    project(w_dt, o_dt)
    if with_gate:
        project(w_gate, refs.pop(0))


def _in_proj(x2, g, sc, sh, seq_len, per_seq_mod, w, lw, tables, with_gate, tm):
    m, d = x2.shape
    rotate = tables is not None
    seq_tiles = seq_len // tm
    mod_map = (lambda i: (i // seq_tiles, 0, 0)) if per_seq_mod else (lambda i: (0, 0, 0))
    weights = [w['na'], w['mla'], w['gqa'], w['xbc'], w['z'], w['dt']] + ([w['gate']] if with_gate else [])
    bd = jnp.asarray(np.kron(np.eye(2), np.ones((HALF, HALF))), BF16)
    nch = w['xbc'].shape[1]
    conv_w = jnp.zeros((8, nch), F32).at[:SSM_CONV].set(lw['ssm_conv_w'])
    params = [lw['mla_g_q'].reshape(1, -1), lw['mla_g_kv'].reshape(1, -1), w['mla_uq'], w['mla_uk'], w['mla_uv'],
              jnp.tile(lw['gqa_g_q'], 2).reshape(1, LANES), jnp.tile(lw['gqa_g_k'], 2).reshape(1, LANES), bd,
              conv_w, lw['ssm_conv_b'].reshape(1, nch)]
    hb = tm // CONV_HALO
    last = m // CONV_HALO - 1
    in_specs = [pl.BlockSpec((tm, d), lambda i: (i, 0)),
                pl.BlockSpec((CONV_HALO, d), lambda i: (jnp.maximum(i * hb - 1, 0), 0)),
                pl.BlockSpec((CONV_HALO, d), lambda i: (jnp.minimum((i + 1) * hb, last), 0)),
                _resident((1, d)), pl.BlockSpec((1, 1, d), mod_map), pl.BlockSpec((1, 1, d), mod_map)]
    in_specs += [_resident(a.shape) for a in weights + params]
    args = [x2, x2, x2, g.reshape(1, d), sc, sh] + weights + params
    if rotate:
        in_specs += [pl.BlockSpec((tm, LANES), lambda i: (i % seq_tiles, 0))] * 4
        args += list(tables['mla']) + list(tables['gqa'])
    outs = [(w['na'].shape[1], BF16),
            (MLA_HEADS * LANES, BF16), (MLA_HEADS * LANES, BF16), (MLA_HEADS * MLA_V, BF16),
            (GQA_Q_HEADS * GQA_HEAD_DIM, BF16), (GQA_KV_HEADS * QUAD_W, BF16), (GQA_KV_HEADS * QUAD_W, BF16),
            (nch, BF16), (w['z'].shape[1], BF16), (w['dt'].shape[1], F32)]
    if with_gate:
        outs.append((w['gate'].shape[1], BF16))
    return pl.pallas_call(
        functools.partial(_in_proj_kernel, with_gate=with_gate, rotate=rotate, seq_tiles=seq_tiles),
        grid=(m // tm,),
        in_specs=in_specs,
        out_specs=[pl.BlockSpec((tm, n), lambda i: (i, 0)) for n, _ in outs],
        out_shape=[jax.ShapeDtypeStruct((m, n), dt) for n, dt in outs],
        compiler_params=_params(1),
        name="in_proj",
    )(*args)


def _na_kernel(q_ref, k_ref, v_ref, kc_ref, vc_ref, bias0_ref, bias1_ref, bias2_ref, o_ref):
    bias_refs = (bias0_ref, bias1_ref, bias2_ref)
    lo = _lane() < HALF
    kc = kc_ref[...]
    vc = vc_ref[...]
    qscale = NA_HEAD_DIM ** -0.5 * LOG2E
    nq = NA_BLK * GRID_W
    nk = NA_WIN_ROWS * GRID_W
    def scores(bi):
        r0, ws = bi * NA_BLK, _na_window_start(bi)
        q = q_ref[r0 * GRID_W:r0 * GRID_W + nq, :]
        q = (q.astype(F32) * qscale).astype(q.dtype)
        kw = k_ref[ws * GRID_W:ws * GRID_W + nk, :]
        zero = jnp.zeros_like(q)
        q2 = jnp.concatenate([jnp.where(lo, q, zero), jnp.where(lo, zero, q)], axis=0)
        return _dot_nt(q2, kw) + bias_refs[_na_block_kind(bi)][0], _dot_nt(q2, kc)

    s_next = scores(0)
    for bi in range(NA_BLOCKS):
        r0, ws = bi * NA_BLK, _na_window_start(bi)
        vw = v_ref[ws * GRID_W:ws * GRID_W + nk, :]
        s_loc, s_ctx = s_next
        if bi + 1 < NA_BLOCKS:
            s_next = scores(bi + 1)
        m = jnp.maximum(jnp.max(s_loc, axis=-1, keepdims=True), jnp.max(s_ctx, axis=-1, keepdims=True))
        p_loc = jnp.exp2(s_loc - m)
        p_ctx = jnp.exp2(s_ctx - m)
        den = jnp.sum(p_loc, axis=-1, keepdims=True) + jnp.sum(p_ctx, axis=-1, keepdims=True)
        o = (_dot(p_loc.astype(BF16), vw) + _dot(p_ctx.astype(BF16), vc)) / den
        o_ref[r0 * GRID_W:r0 * GRID_W + nq, :] = jnp.where(lo, o[:nq], o[nq:]).astype(o_ref.dtype)


def _na_window_start(bi):
    return min(max(bi * NA_BLK - NA_ROW_WIN // 2, 0), GRID_ROWS - NA_WIN_ROWS)


def _na_block_kind(bi):
    return 0 if bi == 0 else (2 if bi == NA_BLOCKS - 1 else 1)


def _na_bias_table(rpb_all):
    n_heads = rpb_all.shape[0] * NA_HEADS
    n_rho = 2 * NA_ROW_WIN - 1
    n_off = 2 * NA_COL_WIN - 1
    w = np.arange(GRID_W)
    j = np.arange(GRID_W)
    cs = np.clip(w - NA_COL_WIN // 2, 0, GRID_W - NA_COL_WIN)
    valid = (j[None, :] >= cs[:, None]) & (j[None, :] < cs[:, None] + NA_COL_WIN)
    r = rpb_all.astype(F32).reshape(n_heads, n_rho, n_off) * LOG2E
    r = jnp.pad(r, ((0, 0), (0, 0), (0, GRID_W)))
    period = n_off + GRID_W
    skew = jnp.broadcast_to(r[:, :, None, :], (n_heads, n_rho, GRID_W, period)).reshape(n_heads, n_rho, GRID_W * period)
    skew = skew[:, :, :GRID_W * (period - 1)].reshape(n_heads, n_rho, GRID_W, period - 1)
    t = jnp.where(valid[None, None], skew[:, :, :, NA_COL_WIN - 1:NA_COL_WIN - 1 + GRID_W], NEG)
    tt = t.transpose(0, 2, 1, 3).reshape(n_heads, GRID_W, n_rho * GRID_W)
    ttp = jnp.pad(tt, ((0, 0), (0, 0), (NA_BLK * GRID_W, NA_BLK * GRID_W)), constant_values=NEG)
    nk = NA_WIN_ROWS * GRID_W
    patterns = {}
    for bi in range(NA_BLOCKS):
        ws = _na_window_start(bi)
        pat = []
        for u in range(NA_BLK):
            r = bi * NA_BLK + u
            rs = min(max(r - NA_ROW_WIN // 2, 0), GRID_ROWS - NA_ROW_WIN)
            rho0 = ws - r + NA_ROW_WIN - 1
            in_win = tuple(0 <= ws + i - rs < NA_ROW_WIN for i in range(NA_WIN_ROWS))
            pat.append((rho0, in_win))
        assert patterns.setdefault(_na_block_kind(bi), pat) == pat
    kinds = []
    for kind in range(3):
        per_u = []
        for rho0, in_win in patterns[kind]:
            start = (rho0 + NA_BLK) * GRID_W
            mask = np.repeat(np.asarray(in_win), GRID_W)
            per_u.append(jnp.where(mask[None, None, :], ttp[:, :, start:start + nk], NEG))
        kinds.append(jnp.stack(per_u, axis=1).reshape(n_heads // 2, 2 * NA_BLK * GRID_W, nk))
    return kinds


def _na_attention(o_na_l, o_na_c, bias_kinds, layer, nb):
    bias_spec = pl.BlockSpec((1, 2 * NA_BLK * GRID_W, NA_WIN_ROWS * GRID_W), lambda b, p: (layer * N_PAIRS + p, 0, 0))
    return pl.pallas_call(
        _na_kernel,
        grid=(nb, N_PAIRS),
        in_specs=[pl.BlockSpec((SEQ, LANES), lambda b, p: (b, p)),
                  pl.BlockSpec((SEQ, LANES), lambda b, p: (b, N_PAIRS + p)),
                  pl.BlockSpec((SEQ, LANES), lambda b, p: (b, 2 * N_PAIRS + p)),
                  pl.BlockSpec((CTX_LEN, LANES), lambda b, p: (b, N_PAIRS + p)),
                  pl.BlockSpec((CTX_LEN, LANES), lambda b, p: (b, 2 * N_PAIRS + p)),
                  bias_spec, bias_spec, bias_spec],
        out_specs=pl.BlockSpec((SEQ, LANES), lambda b, p: (b, p)),
        out_shape=jax.ShapeDtypeStruct((nb * SEQ, NA_HEADS * NA_HEAD_DIM), BF16),
        compiler_params=_params(2),
        name="na_attn",
    )(o_na_l, o_na_l, o_na_l, o_na_c, o_na_c, *bias_kinds)


def _attn_kernel(*refs, n_src, dq, scale):
    q_ref = refs[0]
    o_ref = refs[-1]
    lo = _lane() < HALF
    lane_o = _lane((1, QUAD_W))
    def scores(a):
        c0 = a * LANES if dq == LANES else (a // 2) * LANES
        qa = q_ref[:, c0:c0 + LANES]
        if scale != 1.0:
            qa = (qa.astype(F32) * scale).astype(qa.dtype)
        if dq != LANES:
            zero = jnp.zeros_like(qa)
            qa = jnp.where(lo, qa, zero) if a % 2 == 0 else jnp.where(lo, zero, qa)
        return [_dot_nt(qa, refs[1 + 2 * s][:, c0:c0 + LANES]) for s in range(n_src)]

    ss_next = scores(0)
    for quad in range(ATTN_HEADS // QUAD):
        vsl = slice(quad * QUAD_W, (quad + 1) * QUAD_W)
        outs = []
        for a in range(quad * QUAD, (quad + 1) * QUAD):
            ss = ss_next
            if a + 1 < ATTN_HEADS:
                ss_next = scores(a + 1)
            m = ss[0].max(axis=-1, keepdims=True)
            for sc in ss[1:]:
                m = jnp.maximum(m, sc.max(axis=-1, keepdims=True))
            den = None
            o = None
            for s in range(n_src):
                p = jnp.exp2(ss[s] - m)
                ps = jnp.sum(p, axis=-1, keepdims=True)
                po = _dot(p.astype(BF16), refs[2 + 2 * s][:, vsl])
                den = ps if den is None else den + ps
                o = po if o is None else o + po
            outs.append(o / den)
        out = outs[QUAD - 1]
        for a in range(QUAD - 2, -1, -1):
            out = jnp.where(lane_o < (a + 1) * HALF, outs[a], out)
        o_ref[:, vsl] = out.astype(o_ref.dtype)


def _attention(q, srcs, *, nb, t, dq, tq, scale=1.0):
    nq = t // tq
    in_specs = [pl.BlockSpec((tq, ATTN_HEADS * dq), lambda b, i: (b * nq + i, 0))]
    args = [q]
    for k_arr, v_arr, u, k_off, v_off in srcs:
        in_specs.append(pl.BlockSpec((u, ATTN_HEADS * dq), lambda b, i, k_off=k_off: (b, k_off)))
        in_specs.append(pl.BlockSpec((u, ATTN_HEADS * HALF), lambda b, i, v_off=v_off: (b, v_off)))
        args += [k_arr, v_arr]
    return pl.pallas_call(
        functools.partial(_attn_kernel, n_src=len(srcs), dq=dq, scale=scale),
        grid=(nb, nq),
        in_specs=in_specs,
        out_specs=pl.BlockSpec((tq, ATTN_HEADS * HALF), lambda b, i: (b * nq + i, 0)),
        out_shape=jax.ShapeDtypeStruct((nb * t, ATTN_HEADS * HALF), BF16),
        compiler_params=_params(2),
        name="attn",
    )(*args)


def _mla_prep(x, gq_ref, gkv_ref, wq_ref, wk_ref, wv_ref, cos_ref, sin_ref, q_ref, k_ref, v_ref):
    rotate = cos_ref is not None
    scale = (MLA_NOPE + MLA_ROPE) ** -0.5 * LOG2E
    cq = x[:, :MLA_Q_RANK]
    ckv = x[:, MLA_Q_RANK:MLA_Q_RANK + MLA_KV_RANK]
    kr = x[:, MLA_Q_RANK + MLA_KV_RANK:]
    cqn = _rms(cq, gq_ref[...]).astype(BF16)
    ckvn = _rms(ckv, gkv_ref[...]).astype(BF16)
    lane = _lane()
    first_half = lane < MLA_NOPE + MLA_ROPE // 2

    def rope(t):
        if not rotate:
            return t
        rot = jnp.where(first_half, pltpu.roll(t, LANES - MLA_ROPE // 2, axis=1), pltpu.roll(t, MLA_ROPE // 2, axis=1))
        return t * cos_ref[...] + rot * sin_ref[...]

    kr = rope(kr)
    for h in range(MLA_HEADS):
        sl = slice(h * LANES, (h + 1) * LANES)
        qh = rope(_dot(cqn, wq_ref[:, sl]))
        q_ref[:, sl] = (qh * scale).astype(q_ref.dtype)
        k_ref[:, sl] = (_dot(ckvn, wk_ref[:, sl]) + kr).astype(k_ref.dtype)
    v_ref[...] = _dot(ckvn, wv_ref[...]).astype(v_ref.dtype)


def _gqa_prep(x, gq_ref, gk_ref, bd_ref, cos_ref, sin_ref, q_ref, k_ref, v_ref):
    rotate = cos_ref is not None
    scale = GQA_HEAD_DIM ** -0.5 * LOG2E
    lane = _lane()
    lo = lane < HALF
    first_half = (lane & (GQA_HEAD_DIM - 1)) < GQA_HEAD_DIM // 2

    def headnorm(t, g):
        ss = _dot((t * t).astype(BF16), bd_ref[...]) * (1.0 / GQA_HEAD_DIM)
        return t * lax.rsqrt(ss + EPS) * g

    def rope(t):
        if not rotate:
            return t
        rot = jnp.where(first_half, pltpu.roll(t, LANES - GQA_HEAD_DIM // 2, axis=1), pltpu.roll(t, GQA_HEAD_DIM // 2, axis=1))
        return t * cos_ref[...] + rot * sin_ref[...]

    for j in range(N_PAIRS):
        sl = slice(j * LANES, (j + 1) * LANES)
        q_ref[:, sl] = (rope(headnorm(x[:, sl], gq_ref[...])) * scale).astype(q_ref.dtype)
    nq = GQA_Q_HEADS * GQA_HEAD_DIM
    k = rope(headnorm(x[:, nq:nq + LANES], gk_ref[...]))
    v = x[:, nq + LANES:nq + 2 * LANES]
    for t, ref in ((k, k_ref), (v, v_ref)):
        sw = pltpu.roll(t, HALF, axis=1)
        for g, dup in enumerate((jnp.where(lo, t, sw), jnp.where(lo, sw, t))):
            dup = dup.astype(ref.dtype)
            ref[:, (2 * g) * LANES:(2 * g + 1) * LANES] = dup
            ref[:, (2 * g + 1) * LANES:(2 * g + 2) * LANES] = dup


def _rope_tables(dim, lane_start, group):
    t = jnp.arange(SEQ)
    row = (t // GRID_W).astype(F32)
    col = (t % GRID_W).astype(F32)
    quarter = dim // 4
    inv = ROPE_THETA ** (-jnp.arange(quarter, dtype=F32) / quarter)
    ang = jnp.concatenate([row[:, None] * inv, col[:, None] * inv], axis=-1)
    cos, sin = jnp.cos(ang), jnp.sin(ang)
    cos_g = jnp.ones((SEQ, group), F32).at[:, lane_start:lane_start + dim].set(jnp.concatenate([cos, cos], axis=-1))
    sin_g = jnp.zeros((SEQ, group), F32).at[:, lane_start:lane_start + dim].set(jnp.concatenate([-sin, sin], axis=-1))
    reps = LANES // group
    return jnp.tile(cos_g, (1, reps)), jnp.tile(sin_g, (1, reps))


def _softplus(x):
    return jnp.maximum(x, 0.0) + jnp.log(1.0 + jnp.exp(-jnp.abs(x)))


def _ssd_kernel(*refs, with_y):
    n_dir = 4 if with_y else 3
    dir_refs = (refs[:n_dir], refs[n_dir:2 * n_dir])
    al_ref, bi_ref, h0_ref = refs[2 * n_dir:2 * n_dir + 3]
    if with_y:
        yf_ref, yb_ref, hf_ref, st = refs[2 * n_dir + 3:]
        y_refs = (yf_ref, yb_ref)
    else:
        hf_ref, st = refs[2 * n_dir + 3:]
    j = pl.program_id(1)
    q = SSM_CHUNK

    @pl.when(j == 0)
    def _():
        st[...] = h0_ref[:, 0]

    row = lax.broadcasted_iota(jnp.int32, (q, q), 0)
    col = lax.broadcasted_iota(jnp.int32, (q, q), 1)
    lane = _lane()
    lo = lane < HALF

    def col_of(x, h):
        return jnp.sum(jnp.where(lane == h, x, 0.0), axis=1, keepdims=True)

    decay = []
    for d in range(2):
        dt_ref = dir_refs[d][-1]
        m_qs = (col <= row) if d == 0 else (col >= row)
        dt = _softplus(dt_ref[...] + bi_ref[d])
        da = jnp.where(lane < SSM_HEADS, dt * (-jnp.exp(al_ref[d])), 0.0)
        a_cum_c = jnp.dot(jnp.where(m_qs, 1.0, 0.0), da, precision=HIGHEST, preferred_element_type=F32)
        a_tot = jnp.sum(da, axis=0, keepdims=True)
        decay.append((m_qs, dt, a_cum_c, a_tot, a_cum_c.T))

    for d in range(2):
        if with_y:
            x_ref, b_ref, c_ref, dt_ref = dir_refs[d]
        else:
            x_ref, b_ref, dt_ref = dir_refs[d]
        m_qs, dt, a_cum_c, a_tot, a_cum_r = decay[d]

        x = x_ref[...].astype(F32)
        if with_y:
            cm = c_ref[...]
        for g in range(SSM_GROUPS):
            btg = b_ref[:, g * SSM_STATE:(g + 1) * SSM_STATE].astype(F32).T.astype(BF16)
            if with_y:
                cg = cm[:, g * SSM_STATE:(g + 1) * SSM_STATE]
                cb = _dot(cg, btg)
            for pp in range(2):
                p = 2 * g + pp
                h0, h1 = 2 * p, 2 * p + 1
                ac0, ac1 = col_of(a_cum_c, h0), col_of(a_cum_c, h1)
                tot0, tot1 = col_of(a_tot, h0), col_of(a_tot, h1)
                xdt = x[:, p * LANES:(p + 1) * LANES] * jnp.where(lo, col_of(dt, h0), col_of(dt, h1))
                ht = st[d, p]
                if with_y:
                    xdt_b = xdt.astype(BF16)
                    l0 = jnp.exp(jnp.where(m_qs, ac0 - a_cum_r[h0:h0 + 1, :], NEG))
                    l1 = jnp.exp(jnp.where(m_qs, ac1 - a_cum_r[h1:h1 + 1, :], NEG))
                    y_in = jnp.where(lo, _dot((cb * l0).astype(BF16), xdt_b), _dot((cb * l1).astype(BF16), xdt_b))
                    y_st = _dot(cg, ht.astype(BF16)) * jnp.where(lo, jnp.exp(ac0), jnp.exp(ac1))
                    y_refs[d][:, p * LANES:(p + 1) * LANES] = (y_in + y_st).astype(y_refs[d].dtype)
                dec = jnp.where(lo, jnp.exp(tot0 - ac0), jnp.exp(tot1 - ac1))
                st[d, p] = ht * jnp.where(lo, jnp.exp(tot0), jnp.exp(tot1)) + _dot(btg, (xdt * dec).astype(BF16))

    @pl.when(j == pl.num_programs(1) - 1)
    def _():
        hf_ref[:, 0] = st[...]


def _ssd(xbc, o_dt, a_log, dt_bias, h0, *, nb, t, with_y):
    nc = t // SSM_CHUNK
    pad = LANES - SSM_HEADS
    al_row = jnp.pad(a_log.astype(F32), ((0, 0), (0, pad))).reshape(2, 1, LANES)
    bi_row = jnp.pad(dt_bias.astype(F32), ((0, 0), (0, pad))).reshape(2, 1, LANES)
    x_blk = SSM_INNER // SSM_BC
    in_specs, args = [], []
    for d in range(2):
        cidx = (lambda j: j) if d == 0 else (lambda j: nc - 1 - j)
        in_specs += [pl.BlockSpec((SSM_CHUNK, SSM_INNER), lambda b, j, c=cidx: (b * nc + c(j), 0)),
                     pl.BlockSpec((SSM_CHUNK, SSM_BC), lambda b, j, c=cidx: (b * nc + c(j), x_blk))]
        args += [xbc, xbc]
        if with_y:
            in_specs.append(pl.BlockSpec((SSM_CHUNK, SSM_BC), lambda b, j, c=cidx: (b * nc + c(j), x_blk + 1)))
            args.append(xbc)
        in_specs.append(pl.BlockSpec((SSM_CHUNK, LANES), lambda b, j, c=cidx, d=d: (b * nc + c(j), d)))
        args.append(o_dt)
    st_spec = pl.BlockSpec((2, 1, N_PAIRS, SSM_STATE, LANES), lambda b, j: (0, b, 0, 0, 0))
    in_specs += [_resident(al_row.shape), _resident(bi_row.shape), st_spec]
    args += [al_row, bi_row, h0]
    st_shape = jax.ShapeDtypeStruct((2, nb, N_PAIRS, SSM_STATE, LANES), F32)
    out_specs, out_shape = [st_spec], [st_shape]
    if with_y:
        out_specs = [pl.BlockSpec((SSM_CHUNK, SSM_INNER), lambda b, j: (b * nc + j, 0)),
                     pl.BlockSpec((SSM_CHUNK, SSM_INNER), lambda b, j: (b * nc + nc - 1 - j, 0))] + out_specs
        out_shape = [jax.ShapeDtypeStruct((nb * t, SSM_INNER), BF16)] * 2 + out_shape
    return pl.pallas_call(
        functools.partial(_ssd_kernel, with_y=with_y),
        grid=(nb, nc),
        in_specs=in_specs, out_specs=out_specs, out_shape=out_shape,
        scratch_shapes=[pltpu.VMEM((2, N_PAIRS, SSM_STATE, LANES), F32)],
        compiler_params=_params(2),
        name="ssd",
    )(*args)


def _merge_kernel(yna_ref, ymla_ref, ygqa_ref, yf_ref, yb_ref, xs_ref, z_ref, dsk_ref, gssm_ref,
                  gate_ref, wb_ref, wo_ref, x_ref, gt_ref, g_ref, o_ref):
    y = yf_ref[...].astype(F32) + yb_ref[...].astype(F32) + dsk_ref[...] * xs_ref[...].astype(F32)
    y_ssm = _rms(y * _silu(z_ref[...].astype(F32)), gssm_ref[...]).astype(BF16)
    acc = None
    for k, yk in enumerate((yna_ref[...], ymla_ref[...], ygqa_ref[...], y_ssm)):
        proj = _dot(yk, wb_ref[k])
        gate = jax.nn.sigmoid(gate_ref[:, k * D_MODEL:(k + 1) * D_MODEL].astype(F32))
        acc = gate * proj if acc is None else acc + gate * proj
    mix = _dot(acc.astype(BF16), wo_ref[...])
    o_ref[...] = x_ref[...] + gt_ref[0] * _rms(mix, g_ref[...])


def _merge(ys, y_f, y_b, xbc, z, d_skip, g_norm, gate, wb, wo, x2, gt, g_post, tiles_per_mod, tm=MERGE_TM):
    m, d = x2.shape
    mod_map = (lambda i: (0, 0, 0)) if tiles_per_mod is None else (lambda i: (i // tiles_per_mod, 0, 0))
    dvec = jnp.repeat(d_skip.astype(F32), SSM_HEAD_DIM).reshape(1, SSM_INNER)
    row_blk = pl.BlockSpec((tm, BRANCH_W), lambda i: (i, 0))
    in_specs = [row_blk] * (N_BRANCH - 1) + [row_blk, row_blk, row_blk, row_blk,
                                             _resident((1, SSM_INNER)), _resident((1, SSM_INNER))]
    in_specs += [pl.BlockSpec((tm, N_BRANCH * d), lambda i: (i, 0)),
                 _resident(wb.shape), _resident(wo.shape),
                 pl.BlockSpec((tm, d), lambda i: (i, 0)),
                 pl.BlockSpec((1, 1, d), mod_map), _resident((1, d))]
    return pl.pallas_call(
        _merge_kernel,
        grid=(m // tm,),
        in_specs=in_specs,
        out_specs=pl.BlockSpec((tm, d), lambda i: (i, 0)),
        out_shape=jax.ShapeDtypeStruct((m, d), F32),
        compiler_params=_params(1),
        name="merge",
    )(*ys, y_f, y_b, xbc, z, dvec, g_norm.reshape(1, SSM_INNER), gate, wb, wo, x2, gt, g_post.reshape(1, d))


def _ffn_kernel(x_ref, gpre_ref, sc_ref, sh_ref, w1_ref, w3_ref, w2_ref, gt_ref, gpost_ref, o_ref):
    x = x_ref[...]
    hb = (_rms(x, gpre_ref[...]) * (1.0 + sc_ref[0]) + sh_ref[0]).astype(BF16)
    u = _silu(_dot(hb, w1_ref[...])) * _dot(hb, w3_ref[...])
    f = _dot(u.astype(BF16), w2_ref[...])
    o_ref[...] = x + gt_ref[0] * _rms(f, gpost_ref[...])


def _ffn(x2, g_pre, sc, sh, w1, w3, w2, gt, g_post, tiles_per_mod, tm=MERGE_TM):
    m, d = x2.shape
    mod_map = (lambda i: (0, 0, 0)) if tiles_per_mod is None else (lambda i: (i // tiles_per_mod, 0, 0))
    mod = pl.BlockSpec((1, 1, d), mod_map)
    return pl.pallas_call(
        _ffn_kernel,
        grid=(m // tm,),
        in_specs=[pl.BlockSpec((tm, d), lambda i: (i, 0)), _resident((1, d)), mod, mod,
                  _resident(w1.shape), _resident(w3.shape), _resident(w2.shape), mod, _resident((1, d))],
        out_specs=pl.BlockSpec((tm, d), lambda i: (i, 0)),
        out_shape=jax.ShapeDtypeStruct((m, d), F32),
        compiler_params=_params(1),
        name="ffn",
    )(x2, g_pre.reshape(1, d), sc, sh, w1, w3, w2, gt, g_post.reshape(1, d))


def _split_w_in(w_in):
    off, cols = 0, {}
    for name, w in KEY_SPLITS + QUERY_SPLITS:
        cols[name] = w_in[:, off:off + w]
        off += w
    d = w_in.shape[0]
    z = lambda n: jnp.zeros((d, n), w_in.dtype)
    cat = lambda parts: jnp.concatenate(parts, axis=1).astype(BF16)
    w_na = cat([cols['na_q'], cols['na_k'], cols['na_v']])
    w_mla = cat([cols['mla_cq'], cols['mla_ckv'], z(MLA_NOPE), cols['mla_kr'], z(LANES - MLA_NOPE - MLA_ROPE)])
    w_gqa = cat([cols['gqa_q'], cols['gqa_k'], cols['gqa_v']])
    w_xbc = cat([cols['ssm_x'], cols['ssm_B'], cols['ssm_C']])
    w_z = cols['ssm_z'].astype(BF16)
    w_dt = cat([cols['ssm_dt'][:, :SSM_HEADS], z(LANES - SSM_HEADS), cols['ssm_dt'][:, SSM_HEADS:], z(LANES - SSM_HEADS)])
    w_gate = w_in[:, MIX_COLS:].astype(BF16)
    return dict(na=w_na, mla=w_mla, gqa=w_gqa, xbc=w_xbc, z=w_z, dt=w_dt, gate=w_gate)


def _layer(x2, xc2, mods, mc, lw, tables, layer, nb, need_ctx):
    d = D_MODEL
    tpm = SEQ // MERGE_TM
    chunk = lambda a, k: a[:, k * d:(k + 1) * d].reshape(-1, 1, d)
    sh1, sc1, gt1, sh2, sc2, gt2 = (chunk(mods, k) for k in range(6))
    csh1, csc1, cgt1, csh2, csc2, cgt2 = (chunk(mc, k) for k in range(6))
    w = _split_w_in(lw['w_in'])
    w['mla_uq'] = jnp.pad(lw['mla_w_uq'].reshape(MLA_Q_RANK, MLA_HEADS, MLA_NOPE + MLA_ROPE),
                          ((0, 0), (0, 0), (0, LANES - MLA_NOPE - MLA_ROPE))).reshape(MLA_Q_RANK, MLA_HEADS * LANES)
    wkv = lw['mla_w_ukv'].reshape(MLA_KV_RANK, MLA_HEADS, MLA_NOPE + MLA_V)
    w['mla_uk'] = jnp.pad(wkv[:, :, :MLA_NOPE], ((0, 0), (0, 0), (0, LANES - MLA_NOPE))).reshape(MLA_KV_RANK, MLA_HEADS * LANES)
    w['mla_uv'] = wkv[:, :, MLA_NOPE:].reshape(MLA_KV_RANK, MLA_HEADS * MLA_V)
    o_na, ql, kl, vl, gql, gkl, gvl, xbc_l, z_l, o_dt, gate = _in_proj(
        x2, lw['g_pre1'], sc1, sh1, SEQ, True, w, lw, tables, True, IN_PROJ_TM)
    res = _in_proj(xc2, lw['g_pre1'], csc1, csh1, CTX_LEN, False, w, lw, None, need_ctx, CTX_LEN)
    c_na, qc, kc, vc, gqc, gkc, gvc, xbc_c, z_c, c_dt = res[:10]
    c_gate = res[10] if need_ctx else None

    y_na = _na_attention(o_na, c_na, tables['na_bias'], layer, nb)
    y_mla = _attention(ql, [(kc, vc, CTX_LEN, 0, 0), (kl, vl, SEQ, 0, 0)], nb=nb, t=SEQ, dq=LANES, tq=256)
    y_gqa = _attention(gql, [(gkc, gvc, CTX_LEN, 0, 0), (gkl, gvl, SEQ, 0, 0)], nb=nb, t=SEQ, dq=HALF, tq=256)

    h_zero = jnp.zeros((2, nb, N_PAIRS, SSM_STATE, LANES), F32)
    res_c = _ssd(xbc_c, c_dt, lw['ssm_a_log'], lw['ssm_dt_bias'], h_zero, nb=nb, t=CTX_LEN, with_y=need_ctx)
    h_ctx = res_c[-1]
    yf_l, yb_l, _ = _ssd(xbc_l, o_dt, lw['ssm_a_log'], lw['ssm_dt_bias'], h_ctx, nb=nb, t=SEQ, with_y=True)

    wb = lw['w_branch'].astype(BF16)
    wo = lw['w_out'].astype(BF16)
    w1, w3, w2 = lw['ffn_w1'].astype(BF16), lw['ffn_w3'].astype(BF16), lw['ffn_w2'].astype(BF16)
    x2 = _merge((y_na, y_mla, y_gqa), yf_l, yb_l, xbc_l, z_l, lw['ssm_d'], lw['ssm_g_norm'],
                gate, wb, wo, x2, gt1, lw['g_post1'], tpm)
    x2 = _ffn(x2, lw['g_pre2'], sc2, sh2, w1, w3, w2, gt2, lw['g_post2'], tpm)

    if need_ctx:
        yc_na = _attention(c_na, [(c_na, c_na, CTX_LEN, 1, 2)], nb=nb, t=CTX_LEN, dq=HALF,
                           tq=CTX_LEN, scale=NA_HEAD_DIM ** -0.5 * LOG2E)
        yc_mla = _attention(qc, [(kc, vc, CTX_LEN, 0, 0)], nb=nb, t=CTX_LEN, dq=LANES, tq=CTX_LEN)
        yc_gqa = _attention(gqc, [(gkc, gvc, CTX_LEN, 0, 0)], nb=nb, t=CTX_LEN, dq=HALF, tq=CTX_LEN)
        xc2 = _merge((yc_na, yc_mla, yc_gqa), res_c[0], res_c[1], xbc_c, z_c, lw['ssm_d'], lw['ssm_g_norm'],
                     c_gate, wb, wo, xc2, cgt1, lw['g_post1'], None)
        xc2 = _ffn(xc2, lw['g_pre2'], csc2, csh2, w1, w3, w2, cgt2, lw['g_post2'], None)
    return x2, xc2


def kernel(x, c, ctx, c_ctx, w_ada, b_ada, g_pre1, g_post1, g_pre2, g_post2, w_in, na_rpb, mla_g_q, mla_g_kv, mla_w_uq, mla_w_ukv, gqa_g_q, gqa_g_k, ssm_conv_w, ssm_conv_b, ssm_a_log, ssm_dt_bias, ssm_d, ssm_g_norm, w_branch, w_out, ffn_w1, ffn_w3, ffn_w2):
    nb, s, d = x.shape
    assert s == SEQ and d == D_MODEL and ctx.shape[1] == CTX_LEN
    depth = w_ada.shape[0]
    tables = {'mla': _rope_tables(MLA_ROPE, MLA_NOPE, LANES), 'gqa': _rope_tables(GQA_HEAD_DIM, 0, GQA_HEAD_DIM),
              'na_bias': _na_bias_table(na_rpb)}
    x2 = x.reshape(nb * s, d)
    xc2 = ctx.reshape(nb * CTX_LEN, d)
    n_rows = -(-(nb + 1) // 8) * 8
    c_rows = jnp.zeros((n_rows, d), F32).at[:nb].set(c).at[nb].set(c_ctx)
    stacked = dict(w_in=w_in, na_rpb=na_rpb, mla_g_q=mla_g_q, mla_g_kv=mla_g_kv, mla_w_uq=mla_w_uq,
                   mla_w_ukv=mla_w_ukv, gqa_g_q=gqa_g_q, gqa_g_k=gqa_g_k, ssm_conv_w=ssm_conv_w,
                   ssm_conv_b=ssm_conv_b, ssm_a_log=ssm_a_log, ssm_dt_bias=ssm_dt_bias, ssm_d=ssm_d,
                   ssm_g_norm=ssm_g_norm, w_branch=w_branch, w_out=w_out, ffn_w1=ffn_w1, ffn_w3=ffn_w3,
                   ffn_w2=ffn_w2, g_pre1=g_pre1, g_post1=g_post1, g_pre2=g_pre2, g_post2=g_post2)
    for k in ('w_in', 'mla_w_uq', 'mla_w_ukv', 'w_branch', 'w_out', 'ffn_w1', 'ffn_w3', 'ffn_w2'):
        stacked[k] = stacked[k].astype(BF16)
    for l in range(depth):
        lw = {k: v[l] for k, v in stacked.items()}
        mod_all = _ada(c_rows, w_ada[l], b_ada[l])
        x2, xc2 = _layer(x2, xc2, mod_all[:nb], mod_all[nb:nb + 1], lw, tables, l, nb, l < depth - 1)
    return x2.reshape(nb, s, d)
```
